```python
import math
import jax, jax.numpy as jnp
from jax import lax
import numpy as np

D_MODEL = 4096
BATCH = 4
SEQ = 4096
DEPTH = 4

BRANCH_WIDTH = D_MODEL // 2
N_BRANCHES = 3
LRU_WIDTH = BRANCH_WIDTH
LRU_BLOCKS = 8
LRU_BLOCK = LRU_WIDTH // LRU_BLOCKS
LRU_CONV = 4
LRU_C = 8.0
ATTN_HEAD_DIM = 128
ATTN_HEADS = BRANCH_WIDTH // (2 * ATTN_HEAD_DIM)
ROPE_THETA = 10000.0
Q_BLOCK = 128
SCONV_WIDTH = BRANCH_WIDTH
SCONV_K = 3
N_BRANCH_SLICES = 10
IN_COLS = N_BRANCH_SLICES * BRANCH_WIDTH + N_BRANCHES * D_MODEL
NORM_EPS = 1e-6
SUBLN_EPS = 1e-5
POS_OFFSET_MAX = 1024

kernel_name = "hybrid_rglru_diffattn_shortconv_gated_parallel"


def _rmsnorm(x, w, eps=NORM_EPS):
    x32 = x.astype(jnp.float32)
    y = x32 * lax.rsqrt(jnp.mean(x32 * x32, axis=-1, keepdims=True) + eps)
    return (y * w.astype(jnp.float32)).astype(x.dtype)


def _causal_depthwise_conv(x, w):
    k_width = w.shape[0]
    s_len = x.shape[1]
    xp = jnp.pad(x, ((0, 0), (k_width - 1, 0), (0, 0)))
    y = xp[:, 0:s_len] * w[0]
    for j in range(1, k_width):
        y = y + xp[:, j:j + s_len] * w[j]
    return y


def _lin_combine(c1, c2):
    a1, b1 = c1
    a2, b2 = c2
    return a1 * a2, a2 * b1 + b2


def _rglru_branch(xa, conv_w, conv_b, w_rg, b_rg, w_ig, b_ig, lru_lambda):
    xc = _causal_depthwise_conv(xa, conv_w) + conv_b
    b_, s_, _ = xc.shape
    xb = xc.reshape(b_, s_, LRU_BLOCKS, LRU_BLOCK)
    r = jax.nn.sigmoid(jnp.einsum('bsnc,ncd->bsnd', xb, w_rg).reshape(b_, s_, LRU_WIDTH) + b_rg)
    i = jax.nn.sigmoid(jnp.einsum('bsnc,ncd->bsnd', xb, w_ig).reshape(b_, s_, LRU_WIDTH) + b_ig)
    log_a = LRU_C * r.astype(jnp.float32) * jax.nn.log_sigmoid(lru_lambda.astype(jnp.float32))
    a = jnp.exp(log_a)
    mult = jnp.sqrt(-jnp.expm1(2.0 * log_a))
    u = xc.astype(jnp.float32) * i.astype(jnp.float32) * mult
    _, h = lax.associative_scan(_lin_combine, (a, u), axis=1)
    return h.astype(xa.dtype)


def _rope(t, positions):
    half = ATTN_HEAD_DIM // 2
    inv_freq = ROPE_THETA ** (-jnp.arange(0, half, dtype=jnp.float32) * (2.0 / ATTN_HEAD_DIM))
    ang = positions.astype(jnp.float32)[..., None] * inv_freq
    cos = jnp.cos(ang)[:, :, None, None, :]
    sin = jnp.sin(ang)[:, :, None, None, :]
    t32 = t.astype(jnp.float32)
    t1, t2 = t32[..., :half], t32[..., half:]
    out = jnp.concatenate([t1 * cos - t2 * sin, t2 * cos + t1 * sin], axis=-1)
    return out.astype(t.dtype)


def _diff_attention(q, k, v, positions, lam_q1, lam_k1, lam_q2, lam_k2, subln_w, lam_init):
    b_, s_, _ = q.shape
    q = _rope(q.reshape(b_, s_, ATTN_HEADS, 2, ATTN_HEAD_DIM), positions)
    k = _rope(k.reshape(b_, s_, ATTN_HEADS, 2, ATTN_HEAD_DIM), positions)
    v = v.reshape(b_, s_, ATTN_HEADS, 2 * ATTN_HEAD_DIM)
    lam = (jnp.exp(jnp.sum(lam_q1.astype(jnp.float32) * lam_k1.astype(jnp.float32)))
           - jnp.exp(jnp.sum(lam_q2.astype(jnp.float32) * lam_k2.astype(jnp.float32)))
           + lam_init)
    scale = ATTN_HEAD_DIM ** -0.5
    outs = []
    for blk in range(s_ // Q_BLOCK):
        start = blk * Q_BLOCK
        end = start + Q_BLOCK
        qb = q[:, start:end]
        kb = k[:, :end]
        vb = v[:, :end]
        s = jnp.einsum('bqhcd,bkhcd->bhcqk', qb, kb).astype(jnp.float32) * scale
        mask = jnp.arange(end)[None, :] <= (start + jnp.arange(Q_BLOCK))[:, None]
        s = jnp.where(mask, s, -jnp.inf)
        p = jax.nn.softmax(s, axis=-1)
        pd = p[:, :, 0] - lam * p[:, :, 1]
        outs.append(jnp.einsum('bhqk,bkhe->bqhe', pd.astype(v.dtype), vb))
    o = jnp.concatenate(outs, axis=1)
    o = _rmsnorm(o, subln_w, SUBLN_EPS) * (1.0 - lam_init)
    return o.reshape(b_, s_, ATTN_HEADS * 2 * ATTN_HEAD_DIM)


def _short_conv_branch(cb, cc, cx, conv_w):
    return cb * _causal_depthwise_conv(cc * cx, conv_w)


def _layer(x, positions, layer_idx, norm_w, w_in, gate_b, conv_a_w, conv_a_b, w_rg, b_rg,
           w_ig, b_ig, lru_lambda, lam_q1, lam_k1, lam_q2, lam_k2, subln_w, conv_c_w,
           w_branch, w_out):
    b_, s_, _ = x.shape
    h = _rmsnorm(x, norm_w)
    proj = jnp.einsum('bsd,de->bse', h, w_in)
    split_points = [BRANCH_WIDTH * (j + 1) for j in range(N_BRANCH_SLICES)]
    (ax, ag, q, k, v, bg, cb, cc, cx, cg, gl) = jnp.split(proj, split_points, axis=-1)
    lam_init = 0.8 - 0.6 * math.exp(-0.3 * layer_idx)
    ya = _rglru_branch(ax, conv_a_w, conv_a_b, w_rg, b_rg, w_ig, b_ig, lru_lambda) * jax.nn.silu(ag)
    yb = _diff_attention(q, k, v, positions, lam_q1, lam_k1, lam_q2, lam_k2, subln_w, lam_init) * jax.nn.silu(bg)
    yc = _short_conv_branch(cb, cc, cx, conv_c_w) * jax.nn.silu(cg)
    gates = jax.nn.sigmoid(gl.reshape(b_, s_, N_BRANCHES, D_MODEL) + gate_b)
    merged = (gates[:, :, 0] * jnp.einsum('bsw,wd->bsd', ya, w_branch[0])
              + gates[:, :, 1] * jnp.einsum('bsw,wd->bsd', yb, w_branch[1])
              + gates[:, :, 2] * jnp.einsum('bsw,wd->bsd', yc, w_branch[2]))
    return x + jnp.einsum('bsd,de->bse', merged, w_out)


def setup_inputs(seed: int = 0) -> dict:
    key = jax.random.key(seed)
    ks = jax.random.split(key, 24)
    f32 = jnp.float32
    x = jax.random.normal(ks[0], (BATCH, SEQ, D_MODEL), f32)
    offs = jax.random.randint(ks[1], (BATCH, 1), 0, POS_OFFSET_MAX, dtype=jnp.int32)
    positions = (offs + jnp.arange(SEQ, dtype=jnp.int32)[None, :]).astype(jnp.int32)
    norm_w = 1.0 + 0.02 * jax.random.normal(ks[2], (DEPTH, D_MODEL), f32)
    w_in = jax.random.normal(ks[3], (DEPTH, D_MODEL, IN_COLS), f32) * D_MODEL ** -0.5
    gate_b = 0.02 * jax.random.normal(ks[4], (DEPTH, N_BRANCHES, D_MODEL), f32)
    conv_a_w = jax.random.normal(ks[5], (DEPTH, LRU_CONV, LRU_WIDTH), f32) * LRU_CONV ** -0.5
    conv_a_b = 0.02 * jax.random.normal(ks[6], (DEPTH, LRU_WIDTH), f32)
    w_rg = jax.random.normal(ks[7], (DEPTH, LRU_BLOCKS, LRU_BLOCK, LRU_BLOCK), f32) * LRU_BLOCK ** -0.5
    b_rg = 0.02 * jax.random.normal(ks[8], (DEPTH, LRU_WIDTH), f32)
    w_ig = jax.random.normal(ks[9], (DEPTH, LRU_BLOCKS, LRU_BLOCK, LRU_BLOCK), f32) * LRU_BLOCK ** -0.5
    b_ig = 0.02 * jax.random.normal(ks[10], (DEPTH, LRU_WIDTH), f32)
    a_c = jax.random.uniform(ks[11], (DEPTH, LRU_WIDTH), f32, 0.9, 0.999)
    s_base = a_c ** (1.0 / LRU_C)
    lru_lambda = jnp.log(s_base) - jnp.log1p(-s_base)
    lam_q1 = 0.1 * jax.random.normal(ks[12], (DEPTH, ATTN_HEAD_DIM), f32)
    lam_k1 = 0.1 * jax.random.normal(ks[13], (DEPTH, ATTN_HEAD_DIM), f32)
    lam_q2 = 0.1 * jax.random.normal(ks[14], (DEPTH, ATTN_HEAD_DIM), f32)
    lam_k2 = 0.1 * jax.random.normal(ks[15], (DEPTH, ATTN_HEAD_DIM), f32)
    subln_w = 1.0 + 0.02 * jax.random.normal(ks[16], (DEPTH, 2 * ATTN_HEAD_DIM), f32)
    conv_c_w = jax.random.normal(ks[17], (DEPTH, SCONV_K, SCONV_WIDTH), f32) * SCONV_K ** -0.5
    w_branch = jax.random.normal(ks[18], (DEPTH, N_BRANCHES, BRANCH_WIDTH, D_MODEL), f32) * BRANCH_WIDTH ** -0.5
    w_out = jax.random.normal(ks[19], (DEPTH, D_MODEL, D_MODEL), f32) * D_MODEL ** -0.5
    final_norm_w = 1.0 + 0.02 * jax.random.normal(ks[20], (D_MODEL,), f32)
    return {"x": x, "positions": positions, "norm_w": norm_w, "w_in": w_in, "gate_b": gate_b,
            "conv_a_w": conv_a_w, "conv_a_b": conv_a_b, "w_rg": w_rg, "b_rg": b_rg,
            "w_ig": w_ig, "b_ig": b_ig, "lru_lambda": lru_lambda, "lam_q1": lam_q1,
            "lam_k1": lam_k1, "lam_q2": lam_q2, "lam_k2": lam_k2, "subln_w": subln_w,
            "conv_c_w": conv_c_w, "w_branch": w_branch, "w_out": w_out,
            "final_norm_w": final_norm_w}


def reference(x, positions, norm_w, w_in, gate_b, conv_a_w, conv_a_b, w_rg, b_rg, w_ig, b_ig,
              lru_lambda, lam_q1, lam_k1, lam_q2, lam_k2, subln_w, conv_c_w, w_branch, w_out,
              final_norm_w):
    for l in range(DEPTH):
        x = _layer(x, positions, l, norm_w[l], w_in[l], gate_b[l], conv_a_w[l], conv_a_b[l],
                   w_rg[l], b_rg[l], w_ig[l], b_ig[l], lru_lambda[l], lam_q1[l], lam_k1[l],
                   lam_q2[l], lam_k2[l], subln_w[l], conv_c_w[l], w_branch[l], w_out[l])
    return _rmsnorm(x, final_norm_w)
```

```python
import functools
import math

import jax
import jax.numpy as jnp
from jax import lax
from jax.experimental import pallas as pl
from jax.experimental.pallas import tpu as pltpu

F32 = jnp.float32
BF16 = jnp.bfloat16

HEAD_DIM = 128
VALUE_DIM = 2 * HEAD_DIM
LRU_C = 8.0
LRU_CONV = 4
SCONV_K = 3
ROPE_THETA = 10000.0
NORM_EPS = 1e-6
SUBLN_EPS = 1e-5
N_BRANCHES = 3

V7X_VMEM_BYTES = 64 * 1024 * 1024
VMEM_LIMIT = V7X_VMEM_BYTES - 8 * 1024 * 1024
SUBLANES = 8
LANES = 128
MASK_VALUE = -1e30


def _cparams(semantics):
    return pltpu.CompilerParams(dimension_semantics=semantics, vmem_limit_bytes=VMEM_LIMIT)


def _sigmoid(x):
    return 1.0 / (1.0 + jnp.exp(-x))


def _silu(x):
    return x * _sigmoid(x)


def _one_minus_exp(z, exp_neg_z):
    poly = 1.0 - z * (1.0 / 8.0)
    for k in range(7, 1, -1):
        poly = 1.0 - (z * (1.0 / k)) * poly
    return jnp.where(z < 0.25, z * poly, 1.0 - exp_neg_z)


def _rmsnorm_kernel(x_ref, w_ref, o_ref, *, eps):
    x = x_ref[...].astype(F32)
    ms = jnp.mean(x * x, axis=-1, keepdims=True)
    o_ref[...] = (x * lax.rsqrt(ms + eps) * w_ref[...]).astype(o_ref.dtype)


def _rmsnorm(x, w, out_dtype, eps=NORM_EPS):
    m, d = x.shape
    tr = min(256, m)
    return pl.pallas_call(
        functools.partial(_rmsnorm_kernel, eps=eps),
        grid=(m // tr,),
        in_specs=[pl.BlockSpec((tr, d), lambda i: (i, 0)),
                  pl.BlockSpec((1, d), lambda i: (0, 0))],
        out_specs=pl.BlockSpec((tr, d), lambda i: (i, 0)),
        out_shape=jax.ShapeDtypeStruct((m, d), out_dtype),
        compiler_params=_cparams(("parallel",)),
        name="rmsnorm",
    )(x, w.reshape(1, d))


def _rope_table_kernel(pos_ref, invf_ref, sign_ref, cos_ref, sin_ref):
    ang = pos_ref[...].astype(F32) * invf_ref[...]
    cos_ref[...] = jnp.cos(ang)
    sin_ref[...] = jnp.sin(ang) * sign_ref[...]


def _rope_tables(positions):
    m = positions.size
    half = HEAD_DIM // 2
    inv_freq = ROPE_THETA ** (-jnp.arange(0, half, dtype=F32) * (2.0 / HEAD_DIM))
    invf = jnp.concatenate([inv_freq, inv_freq]).reshape(1, HEAD_DIM)
    sign = jnp.concatenate([-jnp.ones((half,), F32), jnp.ones((half,), F32)]).reshape(1, HEAD_DIM)
    tr = min(512, m)
    return pl.pallas_call(
        _rope_table_kernel,
        grid=(m // tr,),
        in_specs=[pl.BlockSpec((tr, 1), lambda i: (i, 0)),
                  pl.BlockSpec((1, HEAD_DIM), lambda i: (0, 0)),
                  pl.BlockSpec((1, HEAD_DIM), lambda i: (0, 0))],
        out_specs=[pl.BlockSpec((tr, HEAD_DIM), lambda i: (i, 0)),
                   pl.BlockSpec((tr, HEAD_DIM), lambda i: (i, 0))],
        out_shape=[jax.ShapeDtypeStruct((m, HEAD_DIM), F32)] * 2,
        compiler_params=_cparams(("parallel",)),
        name="rope_tables",
    )(positions.reshape(m, 1), invf, sign)


def _matmul_kernel(x_ref, w_ref, o_ref):
    o_ref[...] = jnp.dot(x_ref[...], w_ref[...], preferred_element_type=F32).astype(o_ref.dtype)


def _matmul(x, w, out_dtype, tm=1024, tn=1024):
    m, k = x.shape
    n = w.shape[1]
    tm, tn = min(tm, m), min(tn, n)
    return pl.pallas_call(
        _matmul_kernel,
        grid=(m // tm, n // tn),
        in_specs=[pl.BlockSpec((tm, k), lambda i, j: (i, 0)),
                  pl.BlockSpec((k, tn), lambda i, j: (0, j))],
        out_specs=pl.BlockSpec((tm, tn), lambda i, j: (i, j)),
        out_shape=jax.ShapeDtypeStruct((m, n), out_dtype),
        compiler_params=_cparams(("parallel", "parallel")),
        name="in_proj",
    )(x, w)


def _matmul_rope_kernel(x_ref, w_ref, cos_ref, sin_ref, o_ref, *, q_col_blocks, q_scale):
    acc = jnp.dot(x_ref[...], w_ref[...], preferred_element_type=F32)
    scale = jnp.where(pl.program_id(1) < q_col_blocks, q_scale, 1.0).astype(F32)
    cos = cos_ref[...] * scale
    sin = sin_ref[...] * scale
    for c in range(acc.shape[1] // HEAD_DIM):
        t = acc[:, c * HEAD_DIM:(c + 1) * HEAD_DIM]
        swapped = pltpu.roll(t, HEAD_DIM // 2, axis=1)
        o_ref[:, c * HEAD_DIM:(c + 1) * HEAD_DIM] = (t * cos + swapped * sin).astype(o_ref.dtype)


def _matmul_rope(x, w, cos, sin, q_cols, tm=1024, tn=1024):
    m, k = x.shape
    n = w.shape[1]
    tm, tn = min(tm, m), min(tn, n, q_cols)
    return pl.pallas_call(
        functools.partial(_matmul_rope_kernel, q_col_blocks=q_cols // tn, q_scale=HEAD_DIM ** -0.5),
        grid=(m // tm, n // tn),
        in_specs=[pl.BlockSpec((tm, k), lambda i, j: (i, 0)),
                  pl.BlockSpec((k, tn), lambda i, j: (0, j)),
                  pl.BlockSpec((tm, HEAD_DIM), lambda i, j: (i, 0)),
                  pl.BlockSpec((tm, HEAD_DIM), lambda i, j: (i, 0))],
        out_specs=pl.BlockSpec((tm, tn), lambda i, j: (i, j)),
        out_shape=jax.ShapeDtypeStruct((m, n), BF16),
        compiler_params=_cparams(("parallel", "parallel")),
        name="qk_proj_rope",
    )(x, w, cos, sin)


def _matmul_residual_kernel(a_ref, w_ref, r_ref, o_ref):
    o_ref[...] = r_ref[...] + jnp.dot(a_ref[...], w_ref[...], preferred_element_type=F32)


def _matmul_residual(a, w, resid, tm=1024, tn=512):
    m, k = a.shape
    n = w.shape[1]
    tm, tn = min(tm, m), min(tn, n)
    return pl.pallas_call(
        _matmul_residual_kernel,
        grid=(m // tm, n // tn),
        in_specs=[pl.BlockSpec((tm, k), lambda i, j: (i, 0)),
                  pl.BlockSpec((k, tn), lambda i, j: (0, j)),
                  pl.BlockSpec((tm, tn), lambda i, j: (i, j))],
        out_specs=pl.BlockSpec((tm, tn), lambda i, j: (i, j)),
        out_shape=jax.ShapeDtypeStruct((m, n), F32),
        compiler_params=_cparams(("parallel", "parallel")),
        name="out_proj_residual",
    )(a, w, resid)


def _rglru_kernel(ax_ref, ag_ref, cw_ref, cb_ref, wg_ref, brg_ref, big_ref, lam_ref, o_ref,
                  xbuf, a_s, u_s, h_s, state, *, n_blocks, block):
    ts = ax_ref.shape[0]
    halo = SUBLANES

    @pl.when(pl.program_id(1) == 0)
    def _():
        xbuf[0:halo, :] = jnp.zeros((halo, xbuf.shape[1]), F32)
        state[...] = jnp.zeros_like(state)

    xbuf[halo:halo + ts, :] = ax_ref[...].astype(F32)
    for n in range(n_blocks):
        cols = slice(n * block, (n + 1) * block)
        xc = cb_ref[:, cols]
        for j in range(LRU_CONV):
            xc = xc + cw_ref[j:j + 1, cols] * xbuf[pl.ds(halo - (LRU_CONV - 1) + j, ts), cols]
        g = jnp.dot(xc.astype(BF16), wg_ref[n], preferred_element_type=F32)
        r = _sigmoid(g[:, :block] + brg_ref[:, cols])
        i = _sigmoid(g[:, block:] + big_ref[:, cols])
        lam = lam_ref[:, cols]
        log_sig = jnp.minimum(lam, 0.0) - jnp.log1p(jnp.exp(-jnp.abs(lam)))
        log_a = LRU_C * r * log_sig
        a = jnp.exp(log_a)
        a_s[:, cols] = a
        u_s[:, cols] = xc * i * jnp.sqrt(_one_minus_exp(-2.0 * log_a, a * a))
    xbuf[0:halo, :] = xbuf[ts:ts + halo, :]

    def group(gi, h):
        base = pl.multiple_of(gi * SUBLANES, SUBLANES)
        for r8 in range(SUBLANES):
            h = a_s[pl.ds(base + r8, 1), :] * h + u_s[pl.ds(base + r8, 1), :]
            h_s[pl.ds(base + r8, 1), :] = h
        return h

    state[...] = lax.fori_loop(0, ts // SUBLANES, group, state[...])
    o_ref[...] = (h_s[...] * _silu(ag_ref[...].astype(F32))).astype(o_ref.dtype)


def _rglru(p, conv_w, conv_b, w_gates, b_rg, b_ig, lru_lambda, batch, seq, width, ts=256):
    n_blocks, block = w_gates.shape[0], w_gates.shape[1]
    ts = min(ts, seq)
    nt = seq // ts
    row = lambda b, t: b * nt + t
    vec = pl.BlockSpec((1, width), lambda b, t: (0, 0))
    return pl.pallas_call(
        functools.partial(_rglru_kernel, n_blocks=n_blocks, block=block),
        grid=(batch, nt),
        in_specs=[pl.BlockSpec((ts, width), lambda b, t: (row(b, t), 0)),
                  pl.BlockSpec((ts, width), lambda b, t: (row(b, t), 1)),
                  pl.BlockSpec((LRU_CONV, width), lambda b, t: (0, 0)),
                  vec,
                  pl.BlockSpec((n_blocks, block, 2 * block), lambda b, t: (0, 0, 0)),
                  vec, vec, vec],
        out_specs=pl.BlockSpec((ts, width), lambda b, t: (row(b, t), 0)),
        out_shape=jax.ShapeDtypeStruct((batch * seq, width), BF16),
        scratch_shapes=[pltpu.VMEM((ts + SUBLANES, width), F32),
                        pltpu.VMEM((ts, width), F32),
                        pltpu.VMEM((ts, width), F32),
                        pltpu.VMEM((ts, width), F32),
                        pltpu.VMEM((1, width), F32)],
        compiler_params=_cparams(("parallel", "arbitrary")),
        name="rglru",
    )(p, p, conv_w, conv_b.reshape(1, width), w_gates, b_rg.reshape(1, width),
      b_ig.reshape(1, width), lru_lambda.reshape(1, width))


def _diff_attn_kernel(q_ref, k_ref, v_ref, g_ref, lam_ref, sw_ref, o_ref, m_s, l_s, acc_s):
    qi, ki = pl.program_id(2), pl.program_id(3)
    tq, tk = q_ref.shape[0], k_ref.shape[0]

    @pl.when(ki == 0)
    def _():
        m_s[...] = jnp.full_like(m_s, MASK_VALUE)
        l_s[...] = jnp.zeros_like(l_s)
        acc_s[...] = jnp.zeros_like(acc_s)

    def step(masked):
        v = v_ref[...]
        if masked:
            rows = lax.broadcasted_iota(jnp.int32, (tq, tk), 0)
            cols = lax.broadcasted_iota(jnp.int32, (tq, tk), 1)
            keep = cols <= rows
        for c in range(2):
            hd = slice(c * HEAD_DIM, (c + 1) * HEAD_DIM)
            s = lax.dot_general(q_ref[:, hd], k_ref[:, hd], (((1,), (1,)), ((), ())),
                                preferred_element_type=F32)
            if masked:
                s = jnp.where(keep, s, MASK_VALUE)
            m_prev = m_s[c]
            m_new = jnp.maximum(m_prev, jnp.max(s, axis=-1, keepdims=True))
            alpha = jnp.exp(m_prev - m_new)
            p = jnp.exp(s - m_new)
            l_s[c] = alpha * l_s[c] + jnp.sum(p, axis=-1, keepdims=True)
            acc_s[c] = alpha * acc_s[c] + jnp.dot(p.astype(BF16), v, preferred_element_type=F32)
            m_s[c] = m_new

    @pl.when(ki < qi)
    def _():
        step(masked=False)

    @pl.when(ki == qi)
    def _():
        step(masked=True)
        lam_init = lam_ref[4:5, 0:1]
        lam = (jnp.exp(jnp.sum(lam_ref[0:1, :] * lam_ref[1:2, :], axis=-1, keepdims=True))
               - jnp.exp(jnp.sum(lam_ref[2:3, :] * lam_ref[3:4, :], axis=-1, keepdims=True))
               + lam_init)
        o = acc_s[0] / l_s[0] - lam * (acc_s[1] / l_s[1])
        o = o * lax.rsqrt(jnp.mean(o * o, axis=-1, keepdims=True) + SUBLN_EPS)
        o = o * sw_ref[...] * (1.0 - lam_init)
        o_ref[...] = (o * _silu(g_ref[...].astype(F32))).astype(o_ref.dtype)


def _diff_attention(qk, p, lam_rows, subln_w, batch, seq, heads, v_col, g_col, tq=512):
    tq = min(tq, seq)
    nq = seq // tq
    return pl.pallas_call(
        _diff_attn_kernel,
        grid=(batch, heads, nq, nq),
        in_specs=[pl.BlockSpec((tq, VALUE_DIM), lambda b, h, i, j: (b * nq + i, h)),
                  pl.BlockSpec((tq, VALUE_DIM), lambda b, h, i, j: (b * nq + jnp.minimum(i, j), heads + h)),
                  pl.BlockSpec((tq, VALUE_DIM), lambda b, h, i, j: (b * nq + jnp.minimum(i, j), v_col + h)),
                  pl.BlockSpec((tq, VALUE_DIM), lambda b, h, i, j: (b * nq + i, g_col + h)),
                  pl.BlockSpec((SUBLANES, HEAD_DIM), lambda b, h, i, j: (0, 0)),
                  pl.BlockSpec((1, VALUE_DIM), lambda b, h, i, j: (0, 0))],
        out_specs=pl.BlockSpec((tq, VALUE_DIM), lambda b, h, i, j: (b * nq + i, h)),
        out_shape=jax.ShapeDtypeStruct((batch * seq, heads * VALUE_DIM), BF16),
        scratch_shapes=[pltpu.VMEM((2, tq, 1), F32),
                        pltpu.VMEM((2, tq, 1), F32),
                        pltpu.VMEM((2, tq, VALUE_DIM), F32)],
        compiler_params=_cparams(("parallel", "parallel", "parallel", "arbitrary")),
        name="diff_attention",
    )(qk, qk, p, p, lam_rows, subln_w.reshape(1, VALUE_DIM))


def _sconv_kernel(b_ref, c_ref, x_ref, g_ref, w_ref, o_ref, zbuf):
    ts = b_ref.shape[0]
    halo = SUBLANES

    @pl.when(pl.program_id(1) == 0)
    def _():
        zbuf[0:halo, :] = jnp.zeros((halo, zbuf.shape[1]), F32)

    zbuf[halo:halo + ts, :] = c_ref[...].astype(F32) * x_ref[...].astype(F32)
    conv = w_ref[0:1, :] * zbuf[pl.ds(halo - (SCONV_K - 1), ts), :]
    for j in range(1, SCONV_K):
        conv = conv + w_ref[j:j + 1, :] * zbuf[pl.ds(halo - (SCONV_K - 1) + j, ts), :]
    o_ref[...] = (b_ref[...].astype(F32) * conv * _silu(g_ref[...].astype(F32))).astype(o_ref.dtype)
    zbuf[0:halo, :] = zbuf[ts:ts + halo, :]


def _short_conv(p, conv_w, batch, seq, width, first_col, ts=256):
    ts = min(ts, seq)
    nt = seq // ts
    spec = lambda c: pl.BlockSpec((ts, width), lambda b, t: (b * nt + t, first_col + c))
    return pl.pallas_call(
        _sconv_kernel,
        grid=(batch, nt),
        in_specs=[spec(0), spec(1), spec(2), spec(3),
                  pl.BlockSpec((SCONV_K, width), lambda b, t: (0, 0))],
        out_specs=pl.BlockSpec((ts, width), lambda b, t: (b * nt + t, 0)),
        out_shape=jax.ShapeDtypeStruct((batch * seq, width), BF16),
        scratch_shapes=[pltpu.VMEM((ts + SUBLANES, width), F32)],
        compiler_params=_cparams(("parallel", "arbitrary")),
        name="short_conv",
    )(p, p, p, p, conv_w)


def _merge_kernel(ya_ref, yb_ref, yc_ref, wa_ref, wb_ref, wc_ref, ga_ref, gb_ref, gc_ref,
                  ba_ref, bb_ref, bc_ref, o_ref):
    out = None
    for y_ref, w_ref, gl_ref, b_ref in ((ya_ref, wa_ref, ga_ref, ba_ref),
                                        (yb_ref, wb_ref, gb_ref, bb_ref),
                                        (yc_ref, wc_ref, gc_ref, bc_ref)):
        gate = _sigmoid(gl_ref[...].astype(F32) + b_ref[...])
        term = gate * jnp.dot(y_ref[...], w_ref[...], preferred_element_type=F32)
        out = term if out is None else out + term
    o_ref[...] = out.astype(o_ref.dtype)


def _merge(ya, yb, yc, w_branch, p, gate_b, gl_col, tm=1024, tn=512):
    m, width = ya.shape
    d = w_branch.shape[2]
    tm, tn = min(tm, m), min(tn, d)
    y_spec = pl.BlockSpec((tm, width), lambda i, j: (i, 0))
    w_spec = lambda br: pl.BlockSpec((None, width, tn), lambda i, j: (br, 0, j))
    g_spec = lambda br: pl.BlockSpec((tm, tn), lambda i, j: (i, (gl_col + br * d) // tn + j))
    b_spec = lambda br: pl.BlockSpec((None, 1, tn), lambda i, j: (br, 0, j))
    gate_b3 = gate_b.reshape(N_BRANCHES, 1, d)
    return pl.pallas_call(
        _merge_kernel,
        grid=(m // tm, d // tn),
        in_specs=[y_spec, y_spec, y_spec, w_spec(0), w_spec(1), w_spec(2),
                  g_spec(0), g_spec(1), g_spec(2), b_spec(0), b_spec(1), b_spec(2)],
        out_specs=pl.BlockSpec((tm, tn), lambda i, j: (i, j)),
        out_shape=jax.ShapeDtypeStruct((m, d), BF16),
        compiler_params=_cparams(("parallel", "parallel")),
        name="gated_merge",
    )(ya, yb, yc, w_branch, w_branch, w_branch, p, p, p, gate_b3, gate_b3, gate_b3)


def kernel(x, positions, norm_w, w_in, gate_b, conv_a_w, conv_a_b, w_rg, b_rg, w_ig, b_ig,
           lru_lambda, lam_q1, lam_k1, lam_q2, lam_k2, subln_w, conv_c_w, w_branch, w_out,
           final_norm_w):
    batch, seq, d = x.shape
    depth = norm_w.shape[0]
    bw = w_branch.shape[2]
    heads = bw // VALUE_DIM
    m = batch * seq
    assert w_in.shape[2] == 10 * bw + N_BRANCHES * d
    assert seq % SUBLANES == 0 and bw % VALUE_DIM == 0

    cos, sin = _rope_tables(positions)
    xf = x.reshape(m, d)
    for l in range(depth):
        w_l = w_in[l]
        w_qk = w_l[:, 2 * bw:4 * bw].astype(BF16)
        w_rest = jnp.concatenate([w_l[:, :2 * bw], w_l[:, 4 * bw:]], axis=1).astype(BF16)
        w_gates = jnp.concatenate([w_rg[l], w_ig[l]], axis=-1).astype(BF16)
        lam_init = 0.8 - 0.6 * math.exp(-0.3 * l)
        lam_rows = jnp.concatenate(
            [lam_q1[l][None], lam_k1[l][None], lam_q2[l][None], lam_k2[l][None],
             jnp.full((1, HEAD_DIM), lam_init, F32), jnp.zeros((SUBLANES - 5, HEAD_DIM), F32)], axis=0)

        h = _rmsnorm(xf, norm_w[l], BF16)
        p = _matmul(h, w_rest, BF16)
        qk = _matmul_rope(h, w_qk, cos, sin, q_cols=bw)

        ya = _rglru(p, conv_a_w[l], conv_a_b[l], w_gates, b_rg[l], b_ig[l], lru_lambda[l],
                    batch, seq, bw)
        yb = _diff_attention(qk, p, lam_rows, subln_w[l], batch, seq, heads,
                             v_col=2 * heads, g_col=3 * heads)
        yc = _short_conv(p, conv_c_w[l], batch, seq, bw, first_col=4)
        merged = _merge(ya, yb, yc, w_branch[l].astype(BF16), p, gate_b[l], gl_col=8 * bw)
        xf = _matmul_residual(merged, w_out[l].astype(BF16), xf)
    return _rmsnorm(xf, final_norm_w, F32).reshape(batch, seq, d)
```

```python
import functools
import math

import jax
import jax.numpy as jnp
from jax import lax
from jax.experimental import pallas as pl
from jax.experimental.pallas import tpu as pltpu

F32 = jnp.float32
BF16 = jnp.bfloat16

HEAD_DIM = 128
VALUE_DIM = 2 * HEAD_DIM
LRU_C = 8.0
LRU_CONV = 4
SCONV_K = 3
ROPE_THETA = 10000.0
NORM_EPS = 1e-6
SUBLN_EPS = 1e-5
N_BRANCHES = 3
N_BRANCH_SLICES = 10

V7X_VMEM_BYTES = 64 * 1024 * 1024
VMEM_LIMIT = V7X_VMEM_BYTES - 8 * 1024 * 1024
SUBLANES = 8
LANES = 128
MASK_VALUE = -1e30
LOG2_E = 1.4426950408889634


def _cparams(semantics):
    return pltpu.CompilerParams(dimension_semantics=semantics, vmem_limit_bytes=VMEM_LIMIT)


def _sigmoid(x):
    return 1.0 / (1.0 + jnp.exp(-x))


def _silu(x):
    return x * _sigmoid(x)


def _one_minus_exp(z, exp_neg_z):
    poly = 1.0 - z * (1.0 / 8.0)
    for k in range(7, 1, -1):
        poly = 1.0 - (z * (1.0 / k)) * poly
    return jnp.where(z < 0.25, z * poly, 1.0 - exp_neg_z)


def _rmsnorm_kernel(x_ref, w_ref, o_ref, *, eps):
    x = x_ref[...].astype(F32)
    ms = jnp.mean(x * x, axis=-1, keepdims=True)
    o_ref[...] = (x * lax.rsqrt(ms + eps) * w_ref[...]).astype(o_ref.dtype)


def _rmsnorm(x, w, out_dtype, eps=NORM_EPS):
    m, d = x.shape
    tr = min(256, m)
    return pl.pallas_call(
        functools.partial(_rmsnorm_kernel, eps=eps),
        grid=(m // tr,),
        in_specs=[pl.BlockSpec((tr, d), lambda i: (i, 0)),
                  pl.BlockSpec((1, d), lambda i: (0, 0))],
        out_specs=pl.BlockSpec((tr, d), lambda i: (i, 0)),
        out_shape=jax.ShapeDtypeStruct((m, d), out_dtype),
        compiler_params=_cparams(("parallel",)),
        name="rmsnorm",
    )(x, w.reshape(1, d))


def _rope_table_kernel(pos_ref, invf_ref, sign_ref, cos_ref, sin_ref):
    ang = pos_ref[...].astype(F32) * invf_ref[...]
    cos_ref[...] = jnp.cos(ang)
    sin_ref[...] = jnp.sin(ang) * sign_ref[...]


def _rope_tables(positions):
    m = positions.size
    half = HEAD_DIM // 2
    inv_freq = ROPE_THETA ** (-jnp.arange(0, half, dtype=F32) * (2.0 / HEAD_DIM))
    invf = jnp.concatenate([inv_freq, inv_freq]).reshape(1, HEAD_DIM)
    sign = jnp.concatenate([-jnp.ones((half,), F32), jnp.ones((half,), F32)]).reshape(1, HEAD_DIM)
    tr = min(512, m)
    return pl.pallas_call(
        _rope_table_kernel,
        grid=(m // tr,),
        in_specs=[pl.BlockSpec((tr, 1), lambda i: (i, 0)),
                  pl.BlockSpec((1, HEAD_DIM), lambda i: (0, 0)),
                  pl.BlockSpec((1, HEAD_DIM), lambda i: (0, 0))],
        out_specs=[pl.BlockSpec((tr, HEAD_DIM), lambda i: (i, 0)),
                   pl.BlockSpec((tr, HEAD_DIM), lambda i: (i, 0))],
        out_shape=[jax.ShapeDtypeStruct((m, HEAD_DIM), F32)] * 2,
        compiler_params=_cparams(("parallel",)),
        name="rope_tables",
    )(positions.reshape(m, 1), invf, sign)


def _matmul_kernel(x_ref, w_ref, o_ref):
    o_ref[...] = jnp.dot(x_ref[...], w_ref[...], preferred_element_type=F32).astype(o_ref.dtype)


def _in_proj(x, w_all, layer, skip_start, skip_cols, tm=1024, tn=1024):
    m, k = x.shape
    n = w_all.shape[2] - skip_cols
    tm, tn = min(tm, m), min(tn, n, skip_cols)
    first, skip = skip_start // tn, skip_cols // tn
    return pl.pallas_call(
        _matmul_kernel,
        grid=(m // tm, n // tn),
        in_specs=[pl.BlockSpec((tm, k), lambda i, j: (i, 0)),
                  pl.BlockSpec((None, k, tn), lambda i, j: (layer, 0, jnp.where(j >= first, j + skip, j)))],
        out_specs=pl.BlockSpec((tm, tn), lambda i, j: (i, j)),
        out_shape=jax.ShapeDtypeStruct((m, n), BF16),
        compiler_params=_cparams(("parallel", "parallel")),
        name="in_proj",
    )(x, w_all)


def _matmul_rope_kernel(x_ref, w_ref, cos_ref, sin_ref, o_ref, *, q_col_blocks, q_scale):
    acc = jnp.dot(x_ref[...], w_ref[...], preferred_element_type=F32)
    scale = jnp.where(pl.program_id(1) < q_col_blocks, q_scale, 1.0).astype(F32)
    cos = cos_ref[...] * scale
    sin = sin_ref[...] * scale
    for c in range(acc.shape[1] // HEAD_DIM):
        t = acc[:, c * HEAD_DIM:(c + 1) * HEAD_DIM]
        swapped = pltpu.roll(t, HEAD_DIM // 2, axis=1)
        o_ref[:, c * HEAD_DIM:(c + 1) * HEAD_DIM] = (t * cos + swapped * sin).astype(o_ref.dtype)


def _qk_proj_rope(x, w_all, layer, col_start, q_cols, cos, sin, tm=1024, tn=1024):
    m, k = x.shape
    n = 2 * q_cols
    tm, tn = min(tm, m), min(tn, q_cols)
    first = col_start // tn
    return pl.pallas_call(
        functools.partial(_matmul_rope_kernel, q_col_blocks=q_cols // tn,
                          q_scale=HEAD_DIM ** -0.5 * LOG2_E),
        grid=(m // tm, n // tn),
        in_specs=[pl.BlockSpec((tm, k), lambda i, j: (i, 0)),
                  pl.BlockSpec((None, k, tn), lambda i, j: (layer, 0, first + j)),
                  pl.BlockSpec((tm, HEAD_DIM), lambda i, j: (i, 0)),
                  pl.BlockSpec((tm, HEAD_DIM), lambda i, j: (i, 0))],
        out_specs=pl.BlockSpec((tm, tn), lambda i, j: (i, j)),
        out_shape=jax.ShapeDtypeStruct((m, n), BF16),
        compiler_params=_cparams(("parallel", "parallel")),
        name="qk_proj_rope",
    )(x, w_all, cos, sin)


def _matmul_residual_kernel(a_ref, w_ref, r_ref, o_ref):
    o_ref[...] = r_ref[...] + jnp.dot(a_ref[...], w_ref[...], preferred_element_type=F32)


def _out_proj_residual(a, w_all, layer, resid, tm=1024, tn=512):
    m, k = a.shape
    n = w_all.shape[2]
    tm, tn = min(tm, m), min(tn, n)
    return pl.pallas_call(
        _matmul_residual_kernel,
        grid=(m // tm, n // tn),
        in_specs=[pl.BlockSpec((tm, k), lambda i, j: (i, 0)),
                  pl.BlockSpec((None, k, tn), lambda i, j: (layer, 0, j)),
                  pl.BlockSpec((tm, tn), lambda i, j: (i, j))],
        out_specs=pl.BlockSpec((tm, tn), lambda i, j: (i, j)),
        out_shape=jax.ShapeDtypeStruct((m, n), F32),
        compiler_params=_cparams(("parallel", "parallel")),
        name="out_proj_residual",
    )(a, w_all, resid)


def _rglru_kernel(ax_ref, ag_ref, cw_ref, cb_ref, wg_ref, brg_ref, big_ref, lam_ref, o_ref,
                  xbuf, a_s, u_s, h_s, state, *, n_blocks, block):
    ts = ax_ref.shape[0]
    halo = SUBLANES

    @pl.when(pl.program_id(1) == 0)
    def _():
        xbuf[0:halo, :] = jnp.zeros((halo, xbuf.shape[1]), F32)
        state[...] = jnp.zeros_like(state)

    xbuf[halo:halo + ts, :] = ax_ref[...].astype(F32)
    for n in range(n_blocks):
        cols = slice(n * block, (n + 1) * block)
        xc = cb_ref[:, cols]
        for j in range(LRU_CONV):
            xc = xc + cw_ref[j:j + 1, cols] * xbuf[pl.ds(halo - (LRU_CONV - 1) + j, ts), cols]
        g = jnp.dot(xc.astype(BF16), wg_ref[n], preferred_element_type=F32)
        r = _sigmoid(g[:, :block] + brg_ref[:, cols])
        i = _sigmoid(g[:, block:] + big_ref[:, cols])
        lam = lam_ref[:, cols]
        log_sig = jnp.minimum(lam, 0.0) - jnp.log1p(jnp.exp(-jnp.abs(lam)))
        log_a = LRU_C * r * log_sig
        a = jnp.exp(log_a)
        a_s[:, cols] = a
        u_s[:, cols] = xc * i * jnp.sqrt(_one_minus_exp(-2.0 * log_a, a * a))
    xbuf[0:halo, :] = xbuf[ts:ts + halo, :]

    def group(gi, h):
        base = pl.multiple_of(gi * SUBLANES, SUBLANES)
        for r8 in range(SUBLANES):
            h = a_s[pl.ds(base + r8, 1), :] * h + u_s[pl.ds(base + r8, 1), :]
            h_s[pl.ds(base + r8, 1), :] = h
        return h

    state[...] = lax.fori_loop(0, ts // SUBLANES, group, state[...])
    o_ref[...] = (h_s[...] * _silu(ag_ref[...].astype(F32))).astype(o_ref.dtype)


def _rglru(p, conv_w, conv_b, w_gates, b_rg, b_ig, lru_lambda, batch, seq, width, ts=256):
    n_blocks, block = w_gates.shape[0], w_gates.shape[1]
    ts = min(ts, seq)
    nt = seq // ts
    row = lambda b, t: b * nt + t
    vec = pl.BlockSpec((1, width), lambda b, t: (0, 0))
    return pl.pallas_call(
        functools.partial(_rglru_kernel, n_blocks=n_blocks, block=block),
        grid=(batch, nt),
        in_specs=[pl.BlockSpec((ts, width), lambda b, t: (row(b, t), 0)),
                  pl.BlockSpec((ts, width), lambda b, t: (row(b, t), 1)),
                  pl.BlockSpec((LRU_CONV, width), lambda b, t: (0, 0)),
                  vec,
                  pl.BlockSpec((n_blocks, block, 2 * block), lambda b, t: (0, 0, 0)),
                  vec, vec, vec],
        out_specs=pl.BlockSpec((ts, width), lambda b, t: (row(b, t), 0)),
        out_shape=jax.ShapeDtypeStruct((batch * seq, width), BF16),
        scratch_shapes=[pltpu.VMEM((ts + SUBLANES, width), F32),
                        pltpu.VMEM((ts, width), F32),
                        pltpu.VMEM((ts, width), F32),
                        pltpu.VMEM((ts, width), F32),
                        pltpu.VMEM((1, width), F32)],
        compiler_params=_cparams(("parallel", "arbitrary")),
        name="rglru",
    )(p, p, conv_w, conv_b.reshape(1, width), w_gates, b_rg.reshape(1, width),
      b_ig.reshape(1, width), lru_lambda.reshape(1, width))


def _diff_attn_kernel(q_ref, k_ref, v_ref, g_ref, lam_ref, sw_ref, o_ref, m_s, l_s, acc_s, *, tk):
    qi = pl.program_id(2)
    tq = q_ref.shape[0]
    per_q = tq // tk
    n_lane = tk // LANES

    m_s[...] = jnp.full_like(m_s, MASK_VALUE)
    l_s[...] = jnp.zeros_like(l_s)
    acc_s[...] = jnp.zeros_like(acc_s)

    def chunk(j, diag):
        k0 = pl.multiple_of(j * tk, tk)
        v = v_ref[pl.ds(k0, tk), :]
        if diag is not None:
            rows = lax.broadcasted_iota(jnp.int32, (tq, tk), 0)
            cols = lax.broadcasted_iota(jnp.int32, (tq, tk), 1) + diag * tk
            keep = cols <= rows
        for c in range(2):
            hd = slice(c * HEAD_DIM, (c + 1) * HEAD_DIM)
            s = lax.dot_general(q_ref[:, hd], k_ref[pl.ds(k0, tk), hd], (((1,), (1,)), ((), ())),
                                preferred_element_type=F32)
            if diag is not None:
                s = jnp.where(keep, s, MASK_VALUE)
            m_prev = m_s[c]
            m_new = jnp.maximum(m_prev, jnp.max(s, axis=-1, keepdims=True))
            alpha = jnp.exp2(m_prev - m_new)
            parts = [jnp.exp2(s[:, n * LANES:(n + 1) * LANES] - m_new) for n in range(n_lane)]
            l_new = alpha * l_s[c]
            for part in parts:
                l_new = l_new + part
            l_s[c] = l_new
            p = jnp.concatenate(parts, axis=-1).astype(BF16)
            alpha_wide = jnp.concatenate([alpha] * (VALUE_DIM // LANES), axis=-1)
            acc_s[c] = alpha_wide * acc_s[c] + jnp.dot(p, v, preferred_element_type=F32)
            m_s[c] = m_new

    def below_diagonal(j, carry):
        chunk(j, None)
        return carry

    lax.fori_loop(0, qi * per_q, below_diagonal, 0)
    for d in range(per_q):
        chunk(qi * per_q + d, d)

    lam_init = lam_ref[4:5, 0:1]
    lam = (jnp.exp(jnp.sum(lam_ref[0:1, :] * lam_ref[1:2, :], axis=-1, keepdims=True))
           - jnp.exp(jnp.sum(lam_ref[2:3, :] * lam_ref[3:4, :], axis=-1, keepdims=True))
           + lam_init)
    l0 = jnp.sum(l_s[0], axis=-1, keepdims=True)
    l1 = jnp.sum(l_s[1], axis=-1, keepdims=True)
    o = acc_s[0] / l0 - lam * (acc_s[1] / l1)
    o = o * lax.rsqrt(jnp.mean(o * o, axis=-1, keepdims=True) + SUBLN_EPS)
    o = o * sw_ref[...] * (1.0 - lam_init)
    o_ref[...] = (o * _silu(g_ref[...].astype(F32))).astype(o_ref.dtype)


def _diff_attention(qk, p, lam_rows, subln_w, batch, seq, heads, v_col, g_col, tq=512, tk=512):
    tq = min(tq, seq)
    tk = min(tk, tq)
    nq = seq // tq
    return pl.pallas_call(
        functools.partial(_diff_attn_kernel, tk=tk),
        grid=(batch, heads, nq),
        in_specs=[pl.BlockSpec((tq, VALUE_DIM), lambda b, h, i: (b * nq + i, h)),
                  pl.BlockSpec((seq, VALUE_DIM), lambda b, h, i: (b, heads + h)),
                  pl.BlockSpec((seq, VALUE_DIM), lambda b, h, i: (b, v_col + h)),
                  pl.BlockSpec((tq, VALUE_DIM), lambda b, h, i: (b * nq + i, g_col + h)),
                  pl.BlockSpec((SUBLANES, HEAD_DIM), lambda b, h, i: (0, 0)),
                  pl.BlockSpec((1, VALUE_DIM), lambda b, h, i: (0, 0))],
        out_specs=pl.BlockSpec((tq, VALUE_DIM), lambda b, h, i: (b * nq + i, h)),
        out_shape=jax.ShapeDtypeStruct((batch * seq, heads * VALUE_DIM), BF16),
        scratch_shapes=[pltpu.VMEM((2, tq, LANES), F32),
                        pltpu.VMEM((2, tq, LANES), F32),
                        pltpu.VMEM((2, tq, VALUE_DIM), F32)],
        compiler_params=_cparams(("parallel", "parallel", "arbitrary")),
        name="diff_attention",
    )(qk, qk, p, p, lam_rows, subln_w.reshape(1, VALUE_DIM))


def _sconv_kernel(b_ref, c_ref, x_ref, g_ref, w_ref, o_ref, zbuf):
    ts = b_ref.shape[0]
    halo = SUBLANES

    @pl.when(pl.program_id(1) == 0)
    def _():
        zbuf[0:halo, :] = jnp.zeros((halo, zbuf.shape[1]), F32)

    zbuf[halo:halo + ts, :] = c_ref[...].astype(F32) * x_ref[...].astype(F32)
    conv = w_ref[0:1, :] * zbuf[pl.ds(halo - (SCONV_K - 1), ts), :]
    for j in range(1, SCONV_K):
        conv = conv + w_ref[j:j + 1, :] * zbuf[pl.ds(halo - (SCONV_K - 1) + j, ts), :]
    o_ref[...] = (b_ref[...].astype(F32) * conv * _silu(g_ref[...].astype(F32))).astype(o_ref.dtype)
    zbuf[0:halo, :] = zbuf[ts:ts + halo, :]


def _short_conv(p, conv_w, batch, seq, width, first_col, ts=256):
    ts = min(ts, seq)
    nt = seq // ts
    spec = lambda c: pl.BlockSpec((ts, width), lambda b, t: (b * nt + t, first_col + c))
    return pl.pallas_call(
        _sconv_kernel,
        grid=(batch, nt),
        in_specs=[spec(0), spec(1), spec(2), spec(3),
                  pl.BlockSpec((SCONV_K, width), lambda b, t: (0, 0))],
        out_specs=pl.BlockSpec((ts, width), lambda b, t: (b * nt + t, 0)),
        out_shape=jax.ShapeDtypeStruct((batch * seq, width), BF16),
        scratch_shapes=[pltpu.VMEM((ts + SUBLANES, width), F32)],
        compiler_params=_cparams(("parallel", "arbitrary")),
        name="short_conv",
    )(p, p, p, p, conv_w)


def _merge_kernel(ya_ref, yb_ref, yc_ref, wa_ref, wb_ref, wc_ref, ga_ref, gb_ref, gc_ref,
                  ba_ref, bb_ref, bc_ref, o_ref):
    out = None
    for y_ref, w_ref, gl_ref, b_ref in ((ya_ref, wa_ref, ga_ref, ba_ref),
                                        (yb_ref, wb_ref, gb_ref, bb_ref),
                                        (yc_ref, wc_ref, gc_ref, bc_ref)):
        gate = _sigmoid(gl_ref[...].astype(F32) + b_ref[...])
        term = gate * jnp.dot(y_ref[...], w_ref[...], preferred_element_type=F32)
        out = term if out is None else out + term
    o_ref[...] = out.astype(o_ref.dtype)


def _merge(ya, yb, yc, w_branch_all, layer, p, gate_b, gl_col, tm=1024, tn=512):
    m, width = ya.shape
    d = w_branch_all.shape[2]
    tm, tn = min(tm, m), min(tn, d)
    y_spec = pl.BlockSpec((tm, width), lambda i, j: (i, 0))
    w_spec = lambda br: pl.BlockSpec((None, width, tn), lambda i, j: (layer * N_BRANCHES + br, 0, j))
    g_spec = lambda br: pl.BlockSpec((tm, tn), lambda i, j: (i, (gl_col + br * d) // tn + j))
    b_spec = lambda br: pl.BlockSpec((None, 1, tn), lambda i, j: (br, 0, j))
    gate_b3 = gate_b.reshape(N_BRANCHES, 1, d)
    return pl.pallas_call(
        _merge_kernel,
        grid=(m // tm, d // tn),
        in_specs=[y_spec, y_spec, y_spec, w_spec(0), w_spec(1), w_spec(2),
                  g_spec(0), g_spec(1), g_spec(2), b_spec(0), b_spec(1), b_spec(2)],
        out_specs=pl.BlockSpec((tm, tn), lambda i, j: (i, j)),
        out_shape=jax.ShapeDtypeStruct((m, d), BF16),
        compiler_params=_cparams(("parallel", "parallel")),
        name="gated_merge",
    )(ya, yb, yc, w_branch_all, w_branch_all, w_branch_all, p, p, p, gate_b3, gate_b3, gate_b3)


def kernel(x, positions, norm_w, w_in, gate_b, conv_a_w, conv_a_b, w_rg, b_rg, w_ig, b_ig,
           lru_lambda, lam_q1, lam_k1, lam_q2, lam_k2, subln_w, conv_c_w, w_branch, w_out,
           final_norm_w):
    batch, seq, d = x.shape
    depth = norm_w.shape[0]
    bw = w_branch.shape[2]
    heads = bw // VALUE_DIM
    m = batch * seq
    assert w_in.shape[2] == N_BRANCH_SLICES * bw + N_BRANCHES * d
    assert seq % SUBLANES == 0 and bw % VALUE_DIM == 0

    w_in_bf = w_in.astype(BF16)
    w_branch_bf = w_branch.astype(BF16).reshape(depth * N_BRANCHES, bw, d)
    w_out_bf = w_out.astype(BF16)
    w_gates_bf = jnp.concatenate([w_rg, w_ig], axis=-1).astype(BF16)

    cos, sin = _rope_tables(positions)
    xf = x.reshape(m, d)
    for l in range(depth):
        lam_init = 0.8 - 0.6 * math.exp(-0.3 * l)
        lam_rows = jnp.concatenate(
            [lam_q1[l][None], lam_k1[l][None], lam_q2[l][None], lam_k2[l][None],
             jnp.full((1, HEAD_DIM), lam_init, F32), jnp.zeros((SUBLANES - 5, HEAD_DIM), F32)], axis=0)

        h = _rmsnorm(xf, norm_w[l], BF16)
        p = _in_proj(h, w_in_bf, l, skip_start=2 * bw, skip_cols=2 * bw)
        qk = _qk_proj_rope(h, w_in_bf, l, col_start=2 * bw, q_cols=bw, cos=cos, sin=sin)

        ya = _rglru(p, conv_a_w[l], conv_a_b[l], w_gates_bf[l], b_rg[l], b_ig[l], lru_lambda[l],
                    batch, seq, bw)
        yb = _diff_attention(qk, p, lam_rows, subln_w[l], batch, seq, heads,
                             v_col=2 * heads, g_col=3 * heads)
        yc = _short_conv(p, conv_c_w[l], batch, seq, bw, first_col=4)
        merged = _merge(ya, yb, yc, w_branch_bf, l, p, gate_b[l], gl_col=8 * bw)
        xf = _out_proj_residual(merged, w_out_bf, l, xf)
    return _rmsnorm(xf, final_norm_w, F32).reshape(batch, seq, d)
```

```python
import functools
import math

import jax
import jax.numpy as jnp
from jax import lax
from jax.experimental import pallas as pl
from jax.experimental.pallas import tpu as pltpu

F32 = jnp.float32
BF16 = jnp.bfloat16

HEAD_DIM = 128
VALUE_DIM = 2 * HEAD_DIM
LRU_C = 8.0
LRU_CONV = 4
SCONV_K = 3
ROPE_THETA = 10000.0
NORM_EPS = 1e-6
SUBLN_EPS = 1e-5
N_BRANCHES = 3
N_BRANCH_SLICES = 10

V7X_VMEM_BYTES = 64 * 1024 * 1024
VMEM_LIMIT = V7X_VMEM_BYTES - 8 * 1024 * 1024
SUBLANES = 8
LANES = 128
MASK_VALUE = -1e30
LOG2_E = 1.4426950408889634


def _cparams(semantics):
    return pltpu.CompilerParams(dimension_semantics=semantics, vmem_limit_bytes=VMEM_LIMIT)


def _sigmoid(x):
    return 1.0 / (1.0 + jnp.exp(-x))


def _silu(x):
    return x * _sigmoid(x)


def _one_minus_exp(z, exp_neg_z):
    poly = 1.0 - z * (1.0 / 5.0)
    for k in range(4, 1, -1):
        poly = 1.0 - (z * (1.0 / k)) * poly
    return jnp.where(z < 1.0 / 64.0, z * poly, 1.0 - exp_neg_z)


def _causal_taps(x, prev_tail, conv_w_rows):
    k_width = len(conv_w_rows)
    out = conv_w_rows[k_width - 1] * x
    head_rows = lax.broadcasted_iota(jnp.int32, (SUBLANES, x.shape[1]), 0)
    for shift in range(1, k_width):
        rolled = pltpu.roll(x, shift, axis=0)
        head = jnp.where(head_rows < shift, pltpu.roll(prev_tail, shift, axis=0), rolled[0:SUBLANES])
        shifted = jnp.concatenate([head, rolled[SUBLANES:]], axis=0)
        out = out + conv_w_rows[k_width - 1 - shift] * shifted
    return out


def _rmsnorm_kernel(x_ref, w_ref, o_ref, *, eps):
    x = x_ref[...].astype(F32)
    ms = jnp.mean(x * x, axis=-1, keepdims=True)
    o_ref[...] = (x * lax.rsqrt(ms + eps) * w_ref[...]).astype(o_ref.dtype)


def _rmsnorm(x, w, out_dtype, eps=NORM_EPS):
    m, d = x.shape
    tr = min(256, m)
    return pl.pallas_call(
        functools.partial(_rmsnorm_kernel, eps=eps),
        grid=(m // tr,),
        in_specs=[pl.BlockSpec((tr, d), lambda i: (i, 0)),
                  pl.BlockSpec((1, d), lambda i: (0, 0))],
        out_specs=pl.BlockSpec((tr, d), lambda i: (i, 0)),
        out_shape=jax.ShapeDtypeStruct((m, d), out_dtype),
        compiler_params=_cparams(("parallel",)),
        name="rmsnorm",
    )(x, w.reshape(1, d))


def _rope_table_kernel(pos_ref, invf_ref, sign_ref, cos_ref, sin_ref):
    ang = pos_ref[...].astype(F32) * invf_ref[...]
    cos_ref[...] = jnp.cos(ang)
    sin_ref[...] = jnp.sin(ang) * sign_ref[...]


def _rope_tables(positions):
    m = positions.size
    half = HEAD_DIM // 2
    inv_freq = ROPE_THETA ** (-jnp.arange(0, half, dtype=F32) * (2.0 / HEAD_DIM))
    invf = jnp.concatenate([inv_freq, inv_freq]).reshape(1, HEAD_DIM)
    sign = jnp.concatenate([-jnp.ones((half,), F32), jnp.ones((half,), F32)]).reshape(1, HEAD_DIM)
    tr = min(512, m)
    return pl.pallas_call(
        _rope_table_kernel,
        grid=(m // tr,),
        in_specs=[pl.BlockSpec((tr, 1), lambda i: (i, 0)),
                  pl.BlockSpec((1, HEAD_DIM), lambda i: (0, 0)),
                  pl.BlockSpec((1, HEAD_DIM), lambda i: (0, 0))],
        out_specs=[pl.BlockSpec((tr, HEAD_DIM), lambda i: (i, 0)),
                   pl.BlockSpec((tr, HEAD_DIM), lambda i: (i, 0))],
        out_shape=[jax.ShapeDtypeStruct((m, HEAD_DIM), F32)] * 2,
        compiler_params=_cparams(("parallel",)),
        name="rope_tables",
    )(positions.reshape(m, 1), invf, sign)


def _matmul_kernel(x_ref, w_ref, o_ref):
    o_ref[...] = jnp.dot(x_ref[...], w_ref[...], preferred_element_type=F32).astype(o_ref.dtype)


def _in_proj(x, w_all, layer, skip_start, skip_cols, tm=1024, tn=1024):
    m, k = x.shape
    n = w_all.shape[2] - skip_cols
    tm, tn = min(tm, m), min(tn, n, skip_cols)
    first, skip = skip_start // tn, skip_cols // tn
    return pl.pallas_call(
        _matmul_kernel,
        grid=(m // tm, n // tn),
        in_specs=[pl.BlockSpec((tm, k), lambda i, j: (i, 0)),
                  pl.BlockSpec((None, k, tn), lambda i, j: (layer, 0, jnp.where(j >= first, j + skip, j)))],
        out_specs=pl.BlockSpec((tm, tn), lambda i, j: (i, j)),
        out_shape=jax.ShapeDtypeStruct((m, n), BF16),
        compiler_params=_cparams(("parallel", "parallel")),
        name="in_proj",
    )(x, w_all)


def _matmul_rope_kernel(x_ref, w_ref, cos_ref, sin_ref, o_ref, *, q_col_blocks, q_scale):
    acc = jnp.dot(x_ref[...], w_ref[...], preferred_element_type=F32)
    scale = jnp.where(pl.program_id(1) < q_col_blocks, q_scale, 1.0).astype(F32)
    cos = cos_ref[...] * scale
    sin = sin_ref[...] * scale
    for c in range(acc.shape[1] // HEAD_DIM):
        t = acc[:, c * HEAD_DIM:(c + 1) * HEAD_DIM]
        swapped = pltpu.roll(t, HEAD_DIM // 2, axis=1)
        o_ref[:, c * HEAD_DIM:(c + 1) * HEAD_DIM] = (t * cos + swapped * sin).astype(o_ref.dtype)


def _qk_proj_rope(x, w_all, layer, col_start, q_cols, cos, sin, tm=1024, tn=1024):
    m, k = x.shape
    n = 2 * q_cols
    tm, tn = min(tm, m), min(tn, q_cols)
    first = col_start // tn
    return pl.pallas_call(
        functools.partial(_matmul_rope_kernel, q_col_blocks=q_cols // tn,
                          q_scale=HEAD_DIM ** -0.5 * LOG2_E),
        grid=(m // tm, n // tn),
        in_specs=[pl.BlockSpec((tm, k), lambda i, j: (i, 0)),
                  pl.BlockSpec((None, k, tn), lambda i, j: (layer, 0, first + j)),
                  pl.BlockSpec((tm, HEAD_DIM), lambda i, j: (i, 0)),
                  pl.BlockSpec((tm, HEAD_DIM), lambda i, j: (i, 0))],
        out_specs=pl.BlockSpec((tm, tn), lambda i, j: (i, j)),
        out_shape=jax.ShapeDtypeStruct((m, n), BF16),
        compiler_params=_cparams(("parallel", "parallel")),
        name="qk_proj_rope",
    )(x, w_all, cos, sin)


def _matmul_residual_kernel(a_ref, w_ref, r_ref, o_ref):
    o_ref[...] = r_ref[...] + jnp.dot(a_ref[...], w_ref[...], preferred_element_type=F32)


def _out_proj_residual(a, w_all, layer, resid, tm=1024, tn=512):
    m, k = a.shape
    n = w_all.shape[2]
    tm, tn = min(tm, m), min(tn, n)
    return pl.pallas_call(
        _matmul_residual_kernel,
        grid=(m // tm, n // tn),
        in_specs=[pl.BlockSpec((tm, k), lambda i, j: (i, 0)),
                  pl.BlockSpec((None, k, tn), lambda i, j: (layer, 0, j)),
                  pl.BlockSpec((tm, tn), lambda i, j: (i, j))],
        out_specs=pl.BlockSpec((tm, tn), lambda i, j: (i, j)),
        out_shape=jax.ShapeDtypeStruct((m, n), F32),
        compiler_params=_cparams(("parallel", "parallel")),
        name="out_proj_residual",
    )(a, w_all, resid)


def _rglru_kernel(ax_ref, ag_ref, cw_ref, cb_ref, wg_ref, brg_ref, big_ref, lam_ref, o_ref,
                  tail, a_s, u_s, h_s, state, *, n_blocks, block):
    ts = ax_ref.shape[0]

    @pl.when(pl.program_id(1) == 0)
    def _():
        tail[...] = jnp.zeros_like(tail)
        state[...] = jnp.zeros_like(state)

    for n in range(n_blocks):
        cols = slice(n * block, (n + 1) * block)
        x = ax_ref[:, cols].astype(F32)
        xc = cb_ref[:, cols] + _causal_taps(
            x, tail[:, cols], [cw_ref[j:j + 1, cols] for j in range(LRU_CONV)])
        tail[:, cols] = x[ts - SUBLANES:, :]
        g = jnp.dot(xc.astype(BF16), wg_ref[n], preferred_element_type=F32)
        r = _sigmoid(g[:, :block] + brg_ref[:, cols])
        i = _sigmoid(g[:, block:] + big_ref[:, cols])
        lam = lam_ref[:, cols]
        log_sig = jnp.minimum(lam, 0.0) - jnp.log1p(jnp.exp(-jnp.abs(lam)))
        log_a = LRU_C * r * log_sig
        a = jnp.exp(log_a)
        a_s[:, cols] = a
        u_s[:, cols] = xc * i * jnp.sqrt(_one_minus_exp(-2.0 * log_a, a * a))
    def group(gi, h):
        base = pl.multiple_of(gi * SUBLANES, SUBLANES)
        for r8 in range(SUBLANES):
            h = a_s[pl.ds(base + r8, 1), :] * h + u_s[pl.ds(base + r8, 1), :]
            h_s[pl.ds(base + r8, 1), :] = h
        return h

    state[...] = lax.fori_loop(0, ts // SUBLANES, group, state[...])
    o_ref[...] = (h_s[...] * _silu(ag_ref[...].astype(F32))).astype(o_ref.dtype)


def _rglru(p, conv_w, conv_b, w_gates, b_rg, b_ig, lru_lambda, batch, seq, width, ts=256):
    n_blocks, block = w_gates.shape[0], w_gates.shape[1]
    ts = min(ts, seq)
    nt = seq // ts
    row = lambda b, t: b * nt + t
    vec = pl.BlockSpec((1, width), lambda b, t: (0, 0))
    return pl.pallas_call(
        functools.partial(_rglru_kernel, n_blocks=n_blocks, block=block),
        grid=(batch, nt),
        in_specs=[pl.BlockSpec((ts, width), lambda b, t: (row(b, t), 0)),
                  pl.BlockSpec((ts, width), lambda b, t: (row(b, t), 1)),
                  pl.BlockSpec((LRU_CONV, width), lambda b, t: (0, 0)),
                  vec,
                  pl.BlockSpec((n_blocks, block, 2 * block), lambda b, t: (0, 0, 0)),
                  vec, vec, vec],
        out_specs=pl.BlockSpec((ts, width), lambda b, t: (row(b, t), 0)),
        out_shape=jax.ShapeDtypeStruct((batch * seq, width), BF16),
        scratch_shapes=[pltpu.VMEM((SUBLANES, width), F32),
                        pltpu.VMEM((ts, width), F32),
                        pltpu.VMEM((ts, width), F32),
                        pltpu.VMEM((ts, width), F32),
                        pltpu.VMEM((1, width), F32)],
        compiler_params=_cparams(("parallel", "arbitrary")),
        name="rglru",
    )(p, p, conv_w, conv_b.reshape(1, width), w_gates, b_rg.reshape(1, width),
      b_ig.reshape(1, width), lru_lambda.reshape(1, width))


def _diff_attn_kernel(q_ref, k_ref, v_ref, g_ref, lam_ref, sw_ref, o_ref, m_s, l_s, acc_s, *, tk):
    qi = pl.program_id(2)
    tq = q_ref.shape[0]
    per_q = tq // tk
    n_lane = tk // LANES

    m_s[...] = jnp.full_like(m_s, MASK_VALUE)
    l_s[...] = jnp.zeros_like(l_s)
    acc_s[...] = jnp.zeros_like(acc_s)

    def chunk(j, diag):
        k0 = pl.multiple_of(j * tk, tk)
        v = v_ref[pl.ds(k0, tk), :]
        if diag is not None:
            rows = lax.broadcasted_iota(jnp.int32, (tq, tk), 0)
            cols = lax.broadcasted_iota(jnp.int32, (tq, tk), 1) + diag * tk
            keep = cols <= rows
        for c in range(2):
            hd = slice(c * HEAD_DIM, (c + 1) * HEAD_DIM)
            s = lax.dot_general(q_ref[:, hd], k_ref[pl.ds(k0, tk), hd], (((1,), (1,)), ((), ())),
                                preferred_element_type=F32)
            if diag is not None:
                s = jnp.where(keep, s, MASK_VALUE)
            m_prev = m_s[c]
            m_new = jnp.maximum(m_prev, jnp.max(s, axis=-1, keepdims=True))
            alpha = jnp.exp2(m_prev - m_new)
            parts = [jnp.exp2(s[:, n * LANES:(n + 1) * LANES] - m_new) for n in range(n_lane)]
            l_new = alpha * l_s[c]
            for part in parts:
                l_new = l_new + part
            l_s[c] = l_new
            p = jnp.concatenate(parts, axis=-1).astype(BF16)
            alpha_wide = jnp.concatenate([alpha] * (VALUE_DIM // LANES), axis=-1)
            acc_s[c] = alpha_wide * acc_s[c] + jnp.dot(p, v, preferred_element_type=F32)
            m_s[c] = m_new

    n_full = qi * per_q

    def below_diagonal(jj, carry):
        chunk(2 * jj, None)
        chunk(2 * jj + 1, None)
        return carry

    lax.fori_loop(0, n_full // 2, below_diagonal, 0)

    @pl.when(n_full % 2 == 1)
    def _():
        chunk(n_full - 1, None)

    for d in range(per_q):
        chunk(qi * per_q + d, d)

    lam_init = lam_ref[4:5, 0:1]
    lam = (jnp.exp(jnp.sum(lam_ref[0:1, :] * lam_ref[1:2, :], axis=-1, keepdims=True))
           - jnp.exp(jnp.sum(lam_ref[2:3, :] * lam_ref[3:4, :], axis=-1, keepdims=True))
           + lam_init)
    l0 = jnp.sum(l_s[0], axis=-1, keepdims=True)
    l1 = jnp.sum(l_s[1], axis=-1, keepdims=True)
    o = acc_s[0] / l0 - lam * (acc_s[1] / l1)
    o = o * lax.rsqrt(jnp.mean(o * o, axis=-1, keepdims=True) + SUBLN_EPS)
    o = o * sw_ref[...] * (1.0 - lam_init)
    o_ref[...] = (o * _silu(g_ref[...].astype(F32))).astype(o_ref.dtype)


def _diff_attention(qk, p, lam_rows, subln_w, batch, seq, heads, v_col, g_col, tq=512, tk=512):
    tq = min(tq, seq)
    tk = min(tk, tq)
    nq = seq // tq
    return pl.pallas_call(
        functools.partial(_diff_attn_kernel, tk=tk),
        grid=(batch, heads, nq),
        in_specs=[pl.BlockSpec((tq, VALUE_DIM), lambda b, h, i: (b * nq + i, h)),
                  pl.BlockSpec((seq, VALUE_DIM), lambda b, h, i: (b, heads + h)),
                  pl.BlockSpec((seq, VALUE_DIM), lambda b, h, i: (b, v_col + h)),
                  pl.BlockSpec((tq, VALUE_DIM), lambda b, h, i: (b * nq + i, g_col + h)),
                  pl.BlockSpec((SUBLANES, HEAD_DIM), lambda b, h, i: (0, 0)),
                  pl.BlockSpec((1, VALUE_DIM), lambda b, h, i: (0, 0))],
        out_specs=pl.BlockSpec((tq, VALUE_DIM), lambda b, h, i: (b * nq + i, h)),
        out_shape=jax.ShapeDtypeStruct((batch * seq, heads * VALUE_DIM), BF16),
        scratch_shapes=[pltpu.VMEM((2, tq, LANES), F32),
                        pltpu.VMEM((2, tq, LANES), F32),
                        pltpu.VMEM((2, tq, VALUE_DIM), F32)],
        compiler_params=_cparams(("parallel", "parallel", "arbitrary")),
        name="diff_attention",
    )(qk, qk, p, p, lam_rows, subln_w.reshape(1, VALUE_DIM))


def _sconv_kernel(b_ref, c_ref, x_ref, g_ref, w_ref, o_ref, tail):
    ts = b_ref.shape[0]

    @pl.when(pl.program_id(1) == 0)
    def _():
        tail[...] = jnp.zeros_like(tail)

    z = c_ref[...].astype(F32) * x_ref[...].astype(F32)
    conv = _causal_taps(z, tail[...], [w_ref[j:j + 1, :] for j in range(SCONV_K)])
    tail[...] = z[ts - SUBLANES:, :]
    o_ref[...] = (b_ref[...].astype(F32) * conv * _silu(g_ref[...].astype(F32))).astype(o_ref.dtype)


def _short_conv(p, conv_w, batch, seq, width, first_col, ts=256):
    ts = min(ts, seq)
    nt = seq // ts
    spec = lambda c: pl.BlockSpec((ts, width), lambda b, t: (b * nt + t, first_col + c))
    return pl.pallas_call(
        _sconv_kernel,
        grid=(batch, nt),
        in_specs=[spec(0), spec(1), spec(2), spec(3),
                  pl.BlockSpec((SCONV_K, width), lambda b, t: (0, 0))],
        out_specs=pl.BlockSpec((ts, width), lambda b, t: (b * nt + t, 0)),
        out_shape=jax.ShapeDtypeStruct((batch * seq, width), BF16),
        scratch_shapes=[pltpu.VMEM((SUBLANES, width), F32)],
        compiler_params=_cparams(("parallel", "arbitrary")),
        name="short_conv",
    )(p, p, p, p, conv_w)


def _merge_kernel(ya_ref, yb_ref, yc_ref, wa_ref, wb_ref, wc_ref, ga_ref, gb_ref, gc_ref,
                  ba_ref, bb_ref, bc_ref, o_ref):
    out = None
    for y_ref, w_ref, gl_ref, b_ref in ((ya_ref, wa_ref, ga_ref, ba_ref),
                                        (yb_ref, wb_ref, gb_ref, bb_ref),
                                        (yc_ref, wc_ref, gc_ref, bc_ref)):
        gate = _sigmoid(gl_ref[...].astype(F32) + b_ref[...])
        term = gate * jnp.dot(y_ref[...], w_ref[...], preferred_element_type=F32)
        out = term if out is None else out + term
    o_ref[...] = out.astype(o_ref.dtype)


def _merge(ya, yb, yc, w_branch_all, layer, p, gate_b, gl_col, tm=1024, tn=512):
    m, width = ya.shape
    d = w_branch_all.shape[2]
    tm, tn = min(tm, m), min(tn, d)
    y_spec = pl.BlockSpec((tm, width), lambda i, j: (i, 0))
    w_spec = lambda br: pl.BlockSpec((None, width, tn), lambda i, j: (layer * N_BRANCHES + br, 0, j))
    g_spec = lambda br: pl.BlockSpec((tm, tn), lambda i, j: (i, (gl_col + br * d) // tn + j))
    b_spec = lambda br: pl.BlockSpec((None, 1, tn), lambda i, j: (br, 0, j))
    gate_b3 = gate_b.reshape(N_BRANCHES, 1, d)
    return pl.pallas_call(
        _merge_kernel,
        grid=(m // tm, d // tn),
        in_specs=[y_spec, y_spec, y_spec, w_spec(0), w_spec(1), w_spec(2),
                  g_spec(0), g_spec(1), g_spec(2), b_spec(0), b_spec(1), b_spec(2)],
        out_specs=pl.BlockSpec((tm, tn), lambda i, j: (i, j)),
        out_shape=jax.ShapeDtypeStruct((m, d), BF16),
        compiler_params=_cparams(("parallel", "parallel")),
        name="gated_merge",
    )(ya, yb, yc, w_branch_all, w_branch_all, w_branch_all, p, p, p, gate_b3, gate_b3, gate_b3)


def kernel(x, positions, norm_w, w_in, gate_b, conv_a_w, conv_a_b, w_rg, b_rg, w_ig, b_ig,
           lru_lambda, lam_q1, lam_k1, lam_q2, lam_k2, subln_w, conv_c_w, w_branch, w_out,
           final_norm_w):
    batch, seq, d = x.shape
    depth = norm_w.shape[0]
    bw = w_branch.shape[2]
    heads = bw // VALUE_DIM
    m = batch * seq
    assert w_in.shape[2] == N_BRANCH_SLICES * bw + N_BRANCHES * d
    assert seq % SUBLANES == 0 and bw % VALUE_DIM == 0

    w_in_bf = w_in.astype(BF16)
    w_branch_bf = w_branch.astype(BF16).reshape(depth * N_BRANCHES, bw, d)
    w_out_bf = w_out.astype(BF16)
    w_gates_bf = jnp.concatenate([w_rg, w_ig], axis=-1).astype(BF16)

    cos, sin = _rope_tables(positions)
    xf = x.reshape(m, d)
    for l in range(depth):
        lam_init = 0.8 - 0.6 * math.exp(-0.3 * l)
        lam_rows = jnp.concatenate(
            [lam_q1[l][None], lam_k1[l][None], lam_q2[l][None], lam_k2[l][None],
             jnp.full((1, HEAD_DIM), lam_init, F32), jnp.zeros((SUBLANES - 5, HEAD_DIM), F32)], axis=0)

        h = _rmsnorm(xf, norm_w[l], BF16)
        p = _in_proj(h, w_in_bf, l, skip_start=2 * bw, skip_cols=2 * bw)
        qk = _qk_proj_rope(h, w_in_bf, l, col_start=2 * bw, q_cols=bw, cos=cos, sin=sin)

        ya = _rglru(p, conv_a_w[l], conv_a_b[l], w_gates_bf[l], b_rg[l], b_ig[l], lru_lambda[l],
                    batch, seq, bw)
        yb = _diff_attention(qk, p, lam_rows, subln_w[l], batch, seq, heads,
                             v_col=2 * heads, g_col=3 * heads)
        yc = _short_conv(p, conv_c_w[l], batch, seq, bw, first_col=4)
        merged = _merge(ya, yb, yc, w_branch_bf, l, p, gate_b[l], gl_col=8 * bw)
        xf = _out_proj_residual(merged, w_out_bf, l, xf)
    return _rmsnorm(xf, final_norm_w, F32).reshape(batch, seq, d)
```

```python
import functools
import math
from typing import NamedTuple

import jax
import jax.numpy as jnp
from jax import lax
from jax.experimental import pallas as pl
from jax.experimental.pallas import tpu as pltpu

F32 = jnp.float32
BF16 = jnp.bfloat16

HEAD_DIM = 128
VALUE_DIM = 2 * HEAD_DIM
LRU_C = 8.0
LRU_CONV = 4
SCONV_K = 3
ROPE_THETA = 10000.0
NORM_EPS = 1e-6
SUBLN_EPS = 1e-5
N_BRANCHES = 3
N_BRANCH_SLICES = 10

V7X_VMEM_BYTES = 64 * 1024 * 1024
VMEM_LIMIT = V7X_VMEM_BYTES - 8 * 1024 * 1024
SUBLANES = 8
LANES = 128
MASK_VALUE = -1e30
LOG2_E = 1.4426950408889634
CAST_TILE = (512, 1024)


def _cparams(semantics):
    return pltpu.CompilerParams(dimension_semantics=semantics, vmem_limit_bytes=VMEM_LIMIT)


def _sigmoid(x):
    return 1.0 / (1.0 + jnp.exp(-x))


def _silu(x):
    return x * _sigmoid(x)


def _one_minus_exp(z, exp_neg_z):
    poly = 1.0 - (z * 0.5) * (1.0 - z * (1.0 / 3.0))
    return jnp.where(z < 1.0 / 64.0, z * poly, 1.0 - exp_neg_z)


def _causal_taps(x, prev_tail, conv_w_rows):
    k_width = len(conv_w_rows)
    out = conv_w_rows[k_width - 1] * x
    head_rows = lax.broadcasted_iota(jnp.int32, (SUBLANES, x.shape[1]), 0)
    for shift in range(1, k_width):
        rolled = pltpu.roll(x, shift, axis=0)
        head = jnp.where(head_rows < shift, pltpu.roll(prev_tail, shift, axis=0), rolled[0:SUBLANES])
        shifted = jnp.concatenate([head, rolled[SUBLANES:]], axis=0)
        out = out + conv_w_rows[k_width - 1 - shift] * shifted
    return out


def _rmsnorm_kernel(x_ref, w_ref, o_ref, *, eps):
    x = x_ref[...].astype(F32)
    ms = jnp.mean(x * x, axis=-1, keepdims=True)
    o_ref[...] = (x * lax.rsqrt(ms + eps) * w_ref[...]).astype(o_ref.dtype)


def _rmsnorm(x, w, out_dtype, eps=NORM_EPS):
    m, d = x.shape
    tr = min(256, m)
    return pl.pallas_call(
        functools.partial(_rmsnorm_kernel, eps=eps),
        grid=(m // tr,),
        in_specs=[pl.BlockSpec((tr, d), lambda i: (i, 0)),
                  pl.BlockSpec((1, d), lambda i: (0, 0))],
        out_specs=pl.BlockSpec((tr, d), lambda i: (i, 0)),
        out_shape=jax.ShapeDtypeStruct((m, d), out_dtype),
        compiler_params=_cparams(("parallel",)),
        name="rmsnorm",
    )(x, w.reshape(1, d))


def _rope_table_kernel(pos_ref, invf_ref, sign_ref, cos_ref, sin_ref):
    ang = pos_ref[...].astype(F32) * invf_ref[...]
    cos_ref[...] = jnp.cos(ang)
    sin_ref[...] = jnp.sin(ang) * sign_ref[...]


def _rope_tables(positions):
    m = positions.size
    half = HEAD_DIM // 2
    inv_freq = ROPE_THETA ** (-jnp.arange(0, half, dtype=F32) * (2.0 / HEAD_DIM))
    invf = jnp.concatenate([inv_freq, inv_freq]).reshape(1, HEAD_DIM)
    sign = jnp.concatenate([-jnp.ones((half,), F32), jnp.ones((half,), F32)]).reshape(1, HEAD_DIM)
    tr = min(512, m)
    return pl.pallas_call(
        _rope_table_kernel,
        grid=(m // tr,),
        in_specs=[pl.BlockSpec((tr, 1), lambda i: (i, 0)),
                  pl.BlockSpec((1, HEAD_DIM), lambda i: (0, 0)),
                  pl.BlockSpec((1, HEAD_DIM), lambda i: (0, 0))],
        out_specs=[pl.BlockSpec((tr, HEAD_DIM), lambda i: (i, 0)),
                   pl.BlockSpec((tr, HEAD_DIM), lambda i: (i, 0))],
        out_shape=[jax.ShapeDtypeStruct((m, HEAD_DIM), F32)] * 2,
        compiler_params=_cparams(("parallel",)),
        name="rope_tables",
    )(positions.reshape(m, 1), invf, sign)


class _CastJob(NamedTuple):
    src: jax.Array
    layer: int
    first_step: int
    tile: tuple
    n_col_tiles: int
    n_tiles: int


def _plan_cast_jobs(srcs, layer, n_steps):
    jobs, rest, step = [], [], 0
    for src in srcs:
        _, rows, cols = src.shape
        tr, tc = min(CAST_TILE[0], rows), min(CAST_TILE[1], cols)
        n_tiles = (rows // tr) * (cols // tc)
        if rows % tr or cols % tc or step + n_tiles > n_steps:
            rest.append(src)
            continue
        jobs.append(_CastJob(src, layer, step, (tr, tc), cols // tc, n_tiles))
        step += n_tiles
    return jobs, rest


def _cast_job_specs(job, n_j):
    def tile_index(i, j):
        t = jnp.clip(i * n_j + j - job.first_step, 0, job.n_tiles - 1)
        return t // job.n_col_tiles, t % job.n_col_tiles
    in_spec = pl.BlockSpec((None,) + job.tile, lambda i, j: (job.layer,) + tile_index(i, j))
    out_spec = pl.BlockSpec(job.tile, tile_index)
    return in_spec, out_spec, jax.ShapeDtypeStruct(job.src.shape[1:], BF16)


def _run_cast_jobs(jobs, src_refs, dst_refs):
    step = pl.program_id(0) * pl.num_programs(1) + pl.program_id(1)
    for job, src_ref, dst_ref in zip(jobs, src_refs, dst_refs):
        @pl.when((step >= job.first_step) & (step < job.first_step + job.n_tiles))
        def _():
            dst_ref[...] = src_ref[...].astype(BF16)


def _call_with_cast_jobs(body, n_in, grid, in_specs, out_spec, out_shape, operands, cast_srcs,
                         cast_layer, name):
    jobs, rest = _plan_cast_jobs(cast_srcs, cast_layer, grid[0] * grid[1])
    specs = [_cast_job_specs(job, grid[1]) for job in jobs]

    def kernel_fn(*refs):
        body(*refs[:n_in], refs[n_in + len(jobs)])
        _run_cast_jobs(jobs, refs[n_in:n_in + len(jobs)], refs[n_in + len(jobs) + 1:])

    outs = pl.pallas_call(
        kernel_fn,
        grid=grid,
        in_specs=list(in_specs) + [s[0] for s in specs],
        out_specs=[out_spec] + [s[1] for s in specs],
        out_shape=[out_shape] + [s[2] for s in specs],
        compiler_params=_cparams(("arbitrary", "arbitrary") if jobs else ("parallel", "parallel")),
        name=name,
    )(*operands, *[job.src for job in jobs])
    by_src = {id(job.src): out for job, out in zip(jobs, outs[1:])}
    by_src.update({id(src): src[cast_layer].astype(BF16) for src in rest})
    return outs[0], [by_src[id(src)] for src in cast_srcs]


def _matmul_kernel(x_ref, w_ref, o_ref):
    o_ref[...] = jnp.dot(x_ref[...], w_ref[...], preferred_element_type=F32).astype(o_ref.dtype)


def _in_proj(x, w, skip_start, skip_cols, cast_srcs, cast_layer, tm=1024, tn=1024):
    m, k = x.shape
    n = w.shape[1] - skip_cols
    tm, tn = min(tm, m), min(tn, n, skip_cols)
    first, skip = skip_start // tn, skip_cols // tn
    return _call_with_cast_jobs(
        _matmul_kernel, 2, (m // tm, n // tn),
        [pl.BlockSpec((tm, k), lambda i, j: (i, 0)),
         pl.BlockSpec((k, tn), lambda i, j: (0, jnp.where(j >= first, j + skip, j)))],
        pl.BlockSpec((tm, tn), lambda i, j: (i, j)),
        jax.ShapeDtypeStruct((m, n), BF16),
        (x, w), cast_srcs, cast_layer, "in_proj")


def _matmul_rope_kernel(x_ref, w_ref, cos_ref, sin_ref, o_ref, *, q_col_blocks, q_scale):
    acc = jnp.dot(x_ref[...], w_ref[...], preferred_element_type=F32)
    scale = jnp.where(pl.program_id(1) < q_col_blocks, q_scale, 1.0).astype(F32)
    cos = cos_ref[...] * scale
    sin = sin_ref[...] * scale
    for c in range(acc.shape[1] // HEAD_DIM):
        t = acc[:, c * HEAD_DIM:(c + 1) * HEAD_DIM]
        swapped = pltpu.roll(t, HEAD_DIM // 2, axis=1)
        o_ref[:, c * HEAD_DIM:(c + 1) * HEAD_DIM] = (t * cos + swapped * sin).astype(o_ref.dtype)


def _qk_proj_rope(x, w, col_start, q_cols, cos, sin, tm=1024, tn=1024):
    m, k = x.shape
    n = 2 * q_cols
    tm, tn = min(tm, m), min(tn, q_cols)
    first = col_start // tn
    return pl.pallas_call(
        functools.partial(_matmul_rope_kernel, q_col_blocks=q_cols // tn,
                          q_scale=HEAD_DIM ** -0.5 * LOG2_E),
        grid=(m // tm, n // tn),
        in_specs=[pl.BlockSpec((tm, k), lambda i, j: (i, 0)),
                  pl.BlockSpec((k, tn), lambda i, j: (0, first + j)),
                  pl.BlockSpec((tm, HEAD_DIM), lambda i, j: (i, 0)),
                  pl.BlockSpec((tm, HEAD_DIM), lambda i, j: (i, 0))],
        out_specs=pl.BlockSpec((tm, tn), lambda i, j: (i, j)),
        out_shape=jax.ShapeDtypeStruct((m, n), BF16),
        compiler_params=_cparams(("parallel", "parallel")),
        name="qk_proj_rope",
    )(x, w, cos, sin)


def _matmul_residual_kernel(a_ref, w_ref, r_ref, o_ref):
    o_ref[...] = r_ref[...] + jnp.dot(a_ref[...], w_ref[...], preferred_element_type=F32)


def _out_proj_residual(a, w, resid, cast_srcs, cast_layer, tm=1024, tn=512):
    m, k = a.shape
    n = w.shape[1]
    tm, tn = min(tm, m), min(tn, n)
    return _call_with_cast_jobs(
        _matmul_residual_kernel, 3, (m // tm, n // tn),
        [pl.BlockSpec((tm, k), lambda i, j: (i, 0)),
         pl.BlockSpec((k, tn), lambda i, j: (0, j)),
         pl.BlockSpec((tm, tn), lambda i, j: (i, j))],
        pl.BlockSpec((tm, tn), lambda i, j: (i, j)),
        jax.ShapeDtypeStruct((m, n), F32),
        (a, w, resid), cast_srcs, cast_layer, "out_proj_residual")


def _rglru_kernel(ax_ref, ag_ref, cw_ref, cb_ref, wg_ref, brg_ref, big_ref, lam_ref, o_ref,
                  tail, a_s, u_s, h_s, state, *, n_blocks, block):
    ts = ax_ref.shape[0]

    @pl.when(pl.program_id(1) == 0)
    def _():
        tail[...] = jnp.zeros_like(tail)
        state[...] = jnp.zeros_like(state)

    for n in range(n_blocks):
        cols = slice(n * block, (n + 1) * block)
        x = ax_ref[:, cols].astype(F32)
        xc = cb_ref[:, cols] + _causal_taps(
            x, tail[:, cols], [cw_ref[j:j + 1, cols] for j in range(LRU_CONV)])
        tail[:, cols] = x[ts - SUBLANES:, :]
        g = jnp.dot(xc.astype(BF16), wg_ref[n], preferred_element_type=F32)
        r = _sigmoid(g[:, :block] + brg_ref[:, cols])
        i = _sigmoid(g[:, block:] + big_ref[:, cols])
        lam = lam_ref[:, cols]
        log_sig = jnp.minimum(lam, 0.0) - jnp.log1p(jnp.exp(-jnp.abs(lam)))
        a = jnp.exp2(r * (log_sig * (LRU_C * LOG2_E)))
        a_s[:, cols] = a
        u_s[:, cols] = xc * i * jnp.sqrt(_one_minus_exp(r * (log_sig * (-2.0 * LRU_C)), a * a))

    def group(gi, h):
        base = pl.multiple_of(gi * SUBLANES, SUBLANES)
        for r8 in range(SUBLANES):
            h = a_s[pl.ds(base + r8, 1), :] * h + u_s[pl.ds(base + r8, 1), :]
            h_s[pl.ds(base + r8, 1), :] = h
        return h

    state[...] = lax.fori_loop(0, ts // SUBLANES, group, state[...])
    o_ref[...] = (h_s[...] * _silu(ag_ref[...].astype(F32))).astype(o_ref.dtype)


def _rglru(p, conv_w, conv_b, w_gates, b_rg, b_ig, lru_lambda, batch, seq, width, ts=256):
    n_blocks, block = w_gates.shape[0], w_gates.shape[1]
    ts = min(ts, seq)
    nt = seq // ts
    row = lambda b, t: b * nt + t
    vec = pl.BlockSpec((1, width), lambda b, t: (0, 0))
    return pl.pallas_call(
        functools.partial(_rglru_kernel, n_blocks=n_blocks, block=block),
        grid=(batch, nt),
        in_specs=[pl.BlockSpec((ts, width), lambda b, t: (row(b, t), 0)),
                  pl.BlockSpec((ts, width), lambda b, t: (row(b, t), 1)),
                  pl.BlockSpec((LRU_CONV, width), lambda b, t: (0, 0)),
                  vec,
                  pl.BlockSpec((n_blocks, block, 2 * block), lambda b, t: (0, 0, 0)),
                  vec, vec, vec],
        out_specs=pl.BlockSpec((ts, width), lambda b, t: (row(b, t), 0)),
        out_shape=jax.ShapeDtypeStruct((batch * seq, width), BF16),
        scratch_shapes=[pltpu.VMEM((SUBLANES, width), F32),
                        pltpu.VMEM((ts, width), F32),
                        pltpu.VMEM((ts, width), F32),
                        pltpu.VMEM((ts, width), F32),
                        pltpu.VMEM((1, width), F32)],
        compiler_params=_cparams(("parallel", "arbitrary")),
        name="rglru",
    )(p, p, conv_w, conv_b.reshape(1, width), w_gates, b_rg.reshape(1, width),
      b_ig.reshape(1, width), lru_lambda.reshape(1, width))


def _diff_attn_kernel(q_ref, k_ref, v_ref, g_ref, lam_ref, sw_ref, o_ref, m_s, l_s, acc_s, *, tk):
    qi = pl.program_id(2)
    tq = q_ref.shape[0]
    per_q = tq // tk
    n_lane = tk // LANES

    m_s[...] = jnp.full_like(m_s, MASK_VALUE)
    l_s[...] = jnp.zeros_like(l_s)
    acc_s[...] = jnp.zeros_like(acc_s)

    def chunk(j, diag):
        k0 = pl.multiple_of(j * tk, tk)
        v = v_ref[pl.ds(k0, tk), :]
        if diag is not None:
            rows = lax.broadcasted_iota(jnp.int32, (tq, tk), 0)
            cols = lax.broadcasted_iota(jnp.int32, (tq, tk), 1) + diag * tk
            keep = cols <= rows
        for c in range(2):
            hd = slice(c * HEAD_DIM, (c + 1) * HEAD_DIM)
            s = lax.dot_general(q_ref[:, hd], k_ref[pl.ds(k0, tk), hd], (((1,), (1,)), ((), ())),
                                preferred_element_type=F32)
            if diag is not None:
                s = jnp.where(keep, s, MASK_VALUE)
            m_prev = m_s[c]
            m_new = jnp.maximum(m_prev, jnp.max(s, axis=-1, keepdims=True))
            alpha = jnp.exp2(m_prev - m_new)
            parts = [jnp.exp2(s[:, n * LANES:(n + 1) * LANES] - m_new) for n in range(n_lane)]
            l_new = alpha * l_s[c]
            for part in parts:
                l_new = l_new + part
            l_s[c] = l_new
            p = jnp.concatenate(parts, axis=-1).astype(BF16)
            alpha_wide = jnp.concatenate([alpha] * (VALUE_DIM // LANES), axis=-1)
            acc_s[c] = alpha_wide * acc_s[c] + jnp.dot(p, v, preferred_element_type=F32)
            m_s[c] = m_new

    n_full = qi * per_q

    def below_diagonal(jj, carry):
        chunk(2 * jj, None)
        chunk(2 * jj + 1, None)
        return carry

    lax.fori_loop(0, n_full // 2, below_diagonal, 0)

    @pl.when(n_full % 2 == 1)
    def _():
        chunk(n_full - 1, None)

    for d in range(per_q):
        chunk(qi * per_q + d, d)

    lam_init = lam_ref[4:5, 0:1]
    lam = (jnp.exp(jnp.sum(lam_ref[0:1, :] * lam_ref[1:2, :], axis=-1, keepdims=True))
           - jnp.exp(jnp.sum(lam_ref[2:3, :] * lam_ref[3:4, :], axis=-1, keepdims=True))
           + lam_init)
    l0 = jnp.sum(l_s[0], axis=-1, keepdims=True)
    l1 = jnp.sum(l_s[1], axis=-1, keepdims=True)
    o = acc_s[0] / l0 - lam * (acc_s[1] / l1)
    o = o * lax.rsqrt(jnp.mean(o * o, axis=-1, keepdims=True) + SUBLN_EPS)
    o = o * sw_ref[...] * (1.0 - lam_init)
    o_ref[...] = (o * _silu(g_ref[...].astype(F32))).astype(o_ref.dtype)


def _diff_attention(qk, p, lam_rows, subln_w, batch, seq, heads, v_col, g_col, tq=512, tk=512):
    tq = min(tq, seq)
    tk = min(tk, tq)
    nq = seq // tq
    return pl.pallas_call(
        functools.partial(_diff_attn_kernel, tk=tk),
        grid=(batch, heads, nq),
        in_specs=[pl.BlockSpec((tq, VALUE_DIM), lambda b, h, i: (b * nq + i, h)),
                  pl.BlockSpec((seq, VALUE_DIM), lambda b, h, i: (b, heads + h)),
                  pl.BlockSpec((seq, VALUE_DIM), lambda b, h, i: (b, v_col + h)),
                  pl.BlockSpec((tq, VALUE_DIM), lambda b, h, i: (b * nq + i, g_col + h)),
                  pl.BlockSpec((SUBLANES, HEAD_DIM), lambda b, h, i: (0, 0)),
                  pl.BlockSpec((1, VALUE_DIM), lambda b, h, i: (0, 0))],
        out_specs=pl.BlockSpec((tq, VALUE_DIM), lambda b, h, i: (b * nq + i, h)),
        out_shape=jax.ShapeDtypeStruct((batch * seq, heads * VALUE_DIM), BF16),
        scratch_shapes=[pltpu.VMEM((2, tq, LANES), F32),
                        pltpu.VMEM((2, tq, LANES), F32),
                        pltpu.VMEM((2, tq, VALUE_DIM), F32)],
        compiler_params=_cparams(("parallel", "parallel", "arbitrary")),
        name="diff_attention",
    )(qk, qk, p, p, lam_rows, subln_w.reshape(1, VALUE_DIM))


def _sconv_kernel(b_ref, c_ref, x_ref, g_ref, w_ref, o_ref, tail):
    ts = b_ref.shape[0]

    @pl.when(pl.program_id(1) == 0)
    def _():
        tail[...] = jnp.zeros_like(tail)

    z = c_ref[...].astype(F32) * x_ref[...].astype(F32)
    conv = _causal_taps(z, tail[...], [w_ref[j:j + 1, :] for j in range(SCONV_K)])
    tail[...] = z[ts - SUBLANES:, :]
    o_ref[...] = (b_ref[...].astype(F32) * conv * _silu(g_ref[...].astype(F32))).astype(o_ref.dtype)


def _short_conv(p, conv_w, batch, seq, width, first_col, ts=256):
    ts = min(ts, seq)
    nt = seq // ts
    spec = lambda c: pl.BlockSpec((ts, width), lambda b, t: (b * nt + t, first_col + c))
    return pl.pallas_call(
        _sconv_kernel,
        grid=(batch, nt),
        in_specs=[spec(0), spec(1), spec(2), spec(3),
                  pl.BlockSpec((SCONV_K, width), lambda b, t: (0, 0))],
        out_specs=pl.BlockSpec((ts, width), lambda b, t: (b * nt + t, 0)),
        out_shape=jax.ShapeDtypeStruct((batch * seq, width), BF16),
        scratch_shapes=[pltpu.VMEM((SUBLANES, width), F32)],
        compiler_params=_cparams(("parallel", "arbitrary")),
        name="short_conv",
    )(p, p, p, p, conv_w)


def _merge_kernel(ya_ref, yb_ref, yc_ref, wa_ref, wb_ref, wc_ref, ga_ref, gb_ref, gc_ref,
                  ba_ref, bb_ref, bc_ref, o_ref):
    out = None
    for y_ref, w_ref, gl_ref, b_ref in ((ya_ref, wa_ref, ga_ref, ba_ref),
                                        (yb_ref, wb_ref, gb_ref, bb_ref),
                                        (yc_ref, wc_ref, gc_ref, bc_ref)):
        gate = _sigmoid(gl_ref[...].astype(F32) + b_ref[...])
        term = gate * jnp.dot(y_ref[...], w_ref[...], preferred_element_type=F32)
        out = term if out is None else out + term
    o_ref[...] = out.astype(o_ref.dtype)


def _merge(ya, yb, yc, w_branch, p, gate_b, gl_col, tm=1024, tn=512):
    m, width = ya.shape
    d = w_branch.shape[1]
    tm, tn = min(tm, m), min(tn, d)
    y_spec = pl.BlockSpec((tm, width), lambda i, j: (i, 0))
    w_spec = lambda br: pl.BlockSpec((width, tn), lambda i, j: (br, j))
    g_spec = lambda br: pl.BlockSpec((tm, tn), lambda i, j: (i, (gl_col + br * d) // tn + j))
    b_spec = lambda br: pl.BlockSpec((None, 1, tn), lambda i, j: (br, 0, j))
    gate_b3 = gate_b.reshape(N_BRANCHES, 1, d)
    return pl.pallas_call(
        _merge_kernel,
        grid=(m // tm, d // tn),
        in_specs=[y_spec, y_spec, y_spec, w_spec(0), w_spec(1), w_spec(2),
                  g_spec(0), g_spec(1), g_spec(2), b_spec(0), b_spec(1), b_spec(2)],
        out_specs=pl.BlockSpec((tm, tn), lambda i, j: (i, j)),
        out_shape=jax.ShapeDtypeStruct((m, d), BF16),
        compiler_params=_cparams(("parallel", "parallel")),
        name="gated_merge",
    )(ya, yb, yc, w_branch, w_branch, w_branch, p, p, p, gate_b3, gate_b3, gate_b3)


def kernel(x, positions, norm_w, w_in, gate_b, conv_a_w, conv_a_b, w_rg, b_rg, w_ig, b_ig,
           lru_lambda, lam_q1, lam_k1, lam_q2, lam_k2, subln_w, conv_c_w, w_branch, w_out,
           final_norm_w):
    batch, seq, d = x.shape
    depth = norm_w.shape[0]
    bw = w_branch.shape[2]
    heads = bw // VALUE_DIM
    m = batch * seq
    assert w_in.shape[2] == N_BRANCH_SLICES * bw + N_BRANCHES * d
    assert seq % SUBLANES == 0 and bw % VALUE_DIM == 0

    w_branch_rows = w_branch.reshape(depth, N_BRANCHES * bw, d)
    w_in_l = w_in[0].astype(BF16)
    w_branch_l = w_branch_rows[0].astype(BF16)
    w_out_l = w_out[0].astype(BF16)
    w_gates_bf = jnp.concatenate([w_rg, w_ig], axis=-1).astype(BF16)

    cos, sin = _rope_tables(positions)
    xf = x.reshape(m, d)
    for l in range(depth):
        lam_init = 0.8 - 0.6 * math.exp(-0.3 * l)
        lam_rows = jnp.concatenate(
            [lam_q1[l][None], lam_k1[l][None], lam_q2[l][None], lam_k2[l][None],
             jnp.full((1, HEAD_DIM), lam_init, F32), jnp.zeros((SUBLANES - 5, HEAD_DIM), F32)], axis=0)

        h = _rmsnorm(xf, norm_w[l], BF16)
        last = l == depth - 1
        p, next_w_in = _in_proj(h, w_in_l, skip_start=2 * bw, skip_cols=2 * bw,
                                cast_srcs=[] if last else [w_in], cast_layer=l + 1)
        qk = _qk_proj_rope(h, w_in_l, col_start=2 * bw, q_cols=bw, cos=cos, sin=sin)

        ya = _rglru(p, conv_a_w[l], conv_a_b[l], w_gates_bf[l], b_rg[l], b_ig[l], lru_lambda[l],
                    batch, seq, bw)
        yb = _diff_attention(qk, p, lam_rows, subln_w[l], batch, seq, heads,
                             v_col=2 * heads, g_col=3 * heads)
        yc = _short_conv(p, conv_c_w[l], batch, seq, bw, first_col=4)
        merged = _merge(ya, yb, yc, w_branch_l, p, gate_b[l], gl_col=8 * bw)
        xf, next_w = _out_proj_residual(merged, w_out_l, xf,
                                        cast_srcs=[] if last else [w_branch_rows, w_out],
                                        cast_layer=l + 1)
        if not last:
            (w_in_l,), (w_branch_l, w_out_l) = next_w_in, next_w
    return _rmsnorm(xf, final_norm_w, F32).reshape(batch, seq, d)
```

```python
import functools
import math
from typing import NamedTuple

import jax
import jax.numpy as jnp
from jax import lax
from jax.experimental import pallas as pl
from jax.experimental.pallas import tpu as pltpu

F32 = jnp.float32
BF16 = jnp.bfloat16

HEAD_DIM = 128
VALUE_DIM = 2 * HEAD_DIM
LRU_C = 8.0
LRU_CONV = 4
SCONV_K = 3
ROPE_THETA = 10000.0
NORM_EPS = 1e-6
SUBLN_EPS = 1e-5
N_BRANCHES = 3
N_BRANCH_SLICES = 10

V7X_VMEM_BYTES = 64 * 1024 * 1024
VMEM_LIMIT = V7X_VMEM_BYTES - 8 * 1024 * 1024
SUBLANES = 8
LANES = 128
MASK_VALUE = -1e30
LOG2_E = 1.4426950408889634
CAST_TILE = (512, 1024)


def _cparams(semantics):
    return pltpu.CompilerParams(dimension_semantics=semantics, vmem_limit_bytes=VMEM_LIMIT)


def _sigmoid(x):
    return 1.0 / (1.0 + jnp.exp(-x))


def _silu(x):
    return x * _sigmoid(x)


def _one_minus_exp(z, exp_neg_z):
    poly = 1.0 - (z * 0.5) * (1.0 - z * (1.0 / 3.0))
    return jnp.where(z < 1.0 / 64.0, z * poly, 1.0 - exp_neg_z)


def _causal_taps(x, prev_tail, conv_w_rows):
    k_width = len(conv_w_rows)
    out = conv_w_rows[k_width - 1] * x
    head_rows = lax.broadcasted_iota(jnp.int32, (SUBLANES, x.shape[1]), 0)
    for shift in range(1, k_width):
        rolled = pltpu.roll(x, shift, axis=0)
        head = jnp.where(head_rows < shift, pltpu.roll(prev_tail, shift, axis=0), rolled[0:SUBLANES])
        shifted = jnp.concatenate([head, rolled[SUBLANES:]], axis=0)
        out = out + conv_w_rows[k_width - 1 - shift] * shifted
    return out


def _rmsnorm_kernel(x_ref, w_ref, o_ref, *, eps):
    x = x_ref[...].astype(F32)
    ms = jnp.mean(x * x, axis=-1, keepdims=True)
    o_ref[...] = (x * lax.rsqrt(ms + eps) * w_ref[...]).astype(o_ref.dtype)


def _rmsnorm(x, w, out_dtype, eps=NORM_EPS):
    m, d = x.shape
    tr = min(256, m)
    return pl.pallas_call(
        functools.partial(_rmsnorm_kernel, eps=eps),
        grid=(m // tr,),
        in_specs=[pl.BlockSpec((tr, d), lambda i: (i, 0)),
                  pl.BlockSpec((1, d), lambda i: (0, 0))],
        out_specs=pl.BlockSpec((tr, d), lambda i: (i, 0)),
        out_shape=jax.ShapeDtypeStruct((m, d), out_dtype),
        compiler_params=_cparams(("parallel",)),
        name="rmsnorm",
    )(x, w.reshape(1, d))


def _rope_table_kernel(pos_ref, invf_ref, sign_ref, cos_ref, sin_ref):
    ang = pos_ref[...].astype(F32) * invf_ref[...]
    cos_ref[...] = jnp.cos(ang)
    sin_ref[...] = jnp.sin(ang) * sign_ref[...]


def _rope_tables(positions):
    m = positions.size
    half = HEAD_DIM // 2
    inv_freq = ROPE_THETA ** (-jnp.arange(0, half, dtype=F32) * (2.0 / HEAD_DIM))
    invf = jnp.concatenate([inv_freq, inv_freq]).reshape(1, HEAD_DIM)
    sign = jnp.concatenate([-jnp.ones((half,), F32), jnp.ones((half,), F32)]).reshape(1, HEAD_DIM)
    tr = min(512, m)
    return pl.pallas_call(
        _rope_table_kernel,
        grid=(m // tr,),
        in_specs=[pl.BlockSpec((tr, 1), lambda i: (i, 0)),
                  pl.BlockSpec((1, HEAD_DIM), lambda i: (0, 0)),
                  pl.BlockSpec((1, HEAD_DIM), lambda i: (0, 0))],
        out_specs=[pl.BlockSpec((tr, HEAD_DIM), lambda i: (i, 0)),
                   pl.BlockSpec((tr, HEAD_DIM), lambda i: (i, 0))],
        out_shape=[jax.ShapeDtypeStruct((m, HEAD_DIM), F32)] * 2,
        compiler_params=_cparams(("parallel",)),
        name="rope_tables",
    )(positions.reshape(m, 1), invf, sign)


class _CastJob(NamedTuple):
    src: jax.Array
    layer: int
    first_step: int
    tile: tuple
    n_col_tiles: int
    n_tiles: int


def _plan_cast_jobs(srcs, layer, n_steps):
    jobs, rest, step = [], [], 0
    for src in srcs:
        _, rows, cols = src.shape
        tr, tc = min(CAST_TILE[0], rows), min(CAST_TILE[1], cols)
        n_tiles = (rows // tr) * (cols // tc)
        if rows % tr or cols % tc or step + n_tiles > n_steps:
            rest.append(src)
            continue
        jobs.append(_CastJob(src, layer, step, (tr, tc), cols // tc, n_tiles))
        step += n_tiles
    return jobs, rest


def _cast_job_specs(job, n_j):
    def tile_index(i, j):
        t = jnp.clip(i * n_j + j - job.first_step, 0, job.n_tiles - 1)
        return t // job.n_col_tiles, t % job.n_col_tiles
    in_spec = pl.BlockSpec((None,) + job.tile, lambda i, j: (job.layer,) + tile_index(i, j))
    out_spec = pl.BlockSpec(job.tile, tile_index)
    return in_spec, out_spec, jax.ShapeDtypeStruct(job.src.shape[1:], BF16)


def _run_cast_jobs(jobs, src_refs, dst_refs):
    step = pl.program_id(0) * pl.num_programs(1) + pl.program_id(1)
    for job, src_ref, dst_ref in zip(jobs, src_refs, dst_refs):
        @pl.when((step >= job.first_step) & (step < job.first_step + job.n_tiles))
        def _():
            dst_ref[...] = src_ref[...].astype(BF16)


def _call_with_cast_jobs(body, n_in, grid, in_specs, out_spec, out_shape, operands, cast_srcs,
                         cast_layer, name):
    jobs, rest = _plan_cast_jobs(cast_srcs, cast_layer, grid[0] * grid[1])
    specs = [_cast_job_specs(job, grid[1]) for job in jobs]

    def kernel_fn(*refs):
        body(*refs[:n_in], refs[n_in + len(jobs)])
        _run_cast_jobs(jobs, refs[n_in:n_in + len(jobs)], refs[n_in + len(jobs) + 1:])

    outs = pl.pallas_call(
        kernel_fn,
        grid=grid,
        in_specs=list(in_specs) + [s[0] for s in specs],
        out_specs=[out_spec] + [s[1] for s in specs],
        out_shape=[out_shape] + [s[2] for s in specs],
        compiler_params=_cparams(("arbitrary", "arbitrary") if jobs else ("parallel", "parallel")),
        name=name,
    )(*operands, *[job.src for job in jobs])
    by_src = {id(job.src): out for job, out in zip(jobs, outs[1:])}
    by_src.update({id(src): src[cast_layer].astype(BF16) for src in rest})
    return outs[0], [by_src[id(src)] for src in cast_srcs]


def _matmul_kernel(x_ref, w_ref, o_ref):
    o_ref[...] = jnp.dot(x_ref[...], w_ref[...], preferred_element_type=F32).astype(o_ref.dtype)


def _in_proj(x, w, skip_start, skip_cols, cast_srcs, cast_layer, tm=1024, tn=1024):
    m, k = x.shape
    n = w.shape[1] - skip_cols
    tm, tn = min(tm, m), min(tn, n, skip_cols)
    first, skip = skip_start // tn, skip_cols // tn
    return _call_with_cast_jobs(
        _matmul_kernel, 2, (m // tm, n // tn),
        [pl.BlockSpec((tm, k), lambda i, j: (i, 0)),
         pl.BlockSpec((k, tn), lambda i, j: (0, jnp.where(j >= first, j + skip, j)))],
        pl.BlockSpec((tm, tn), lambda i, j: (i, j)),
        jax.ShapeDtypeStruct((m, n), BF16),
        (x, w), cast_srcs, cast_layer, "in_proj")


def _matmul_rope_kernel(x_ref, w_ref, cos_ref, sin_ref, o_ref, *, q_col_blocks, q_scale):
    acc = jnp.dot(x_ref[...], w_ref[...], preferred_element_type=F32)
    scale = jnp.where(pl.program_id(1) < q_col_blocks, q_scale, 1.0).astype(F32)
    cos = cos_ref[...] * scale
    sin = sin_ref[...] * scale
    for c in range(acc.shape[1] // HEAD_DIM):
        t = acc[:, c * HEAD_DIM:(c + 1) * HEAD_DIM]
        swapped = pltpu.roll(t, HEAD_DIM // 2, axis=1)
        o_ref[:, c * HEAD_DIM:(c + 1) * HEAD_DIM] = (t * cos + swapped * sin).astype(o_ref.dtype)


def _qk_proj_rope(x, w, col_start, q_cols, cos, sin, tm=1024, tn=1024):
    m, k = x.shape
    n = 2 * q_cols
    tm, tn = min(tm, m), min(tn, q_cols)
    first = col_start // tn
    return pl.pallas_call(
        functools.partial(_matmul_rope_kernel, q_col_blocks=q_cols // tn,
                          q_scale=HEAD_DIM ** -0.5 * LOG2_E),
        grid=(m // tm, n // tn),
        in_specs=[pl.BlockSpec((tm, k), lambda i, j: (i, 0)),
                  pl.BlockSpec((k, tn), lambda i, j: (0, first + j)),
                  pl.BlockSpec((tm, HEAD_DIM), lambda i, j: (i, 0)),
                  pl.BlockSpec((tm, HEAD_DIM), lambda i, j: (i, 0))],
        out_specs=pl.BlockSpec((tm, tn), lambda i, j: (i, j)),
        out_shape=jax.ShapeDtypeStruct((m, n), BF16),
        compiler_params=_cparams(("parallel", "parallel")),
        name="qk_proj_rope",
    )(x, w, cos, sin)


def _matmul_residual_kernel(a_ref, w_ref, r_ref, o_ref):
    o_ref[...] = r_ref[...] + jnp.dot(a_ref[...], w_ref[...], preferred_element_type=F32)


def _out_proj_residual(a, w, resid, cast_srcs, cast_layer, tm=1024, tn=512):
    m, k = a.shape
    n = w.shape[1]
    tm, tn = min(tm, m), min(tn, n)
    return _call_with_cast_jobs(
        _matmul_residual_kernel, 3, (m // tm, n // tn),
        [pl.BlockSpec((tm, k), lambda i, j: (i, 0)),
         pl.BlockSpec((k, tn), lambda i, j: (0, j)),
         pl.BlockSpec((tm, tn), lambda i, j: (i, j))],
        pl.BlockSpec((tm, tn), lambda i, j: (i, j)),
        jax.ShapeDtypeStruct((m, n), F32),
        (a, w, resid), cast_srcs, cast_layer, "out_proj_residual")


def _rglru_kernel(ax_ref, ag_ref, cw_ref, cb_ref, wg_ref, brg_ref, big_ref, lam_ref, o_ref,
                  tail, a_s, u_s, h_s, state, *, n_blocks, block):
    ts = ax_ref.shape[0]

    @pl.when(pl.program_id(1) == 0)
    def _():
        tail[...] = jnp.zeros_like(tail)
        state[...] = jnp.zeros_like(state)

    for n in range(n_blocks):
        cols = slice(n * block, (n + 1) * block)
        x = ax_ref[:, cols].astype(F32)
        xc = cb_ref[:, cols] + _causal_taps(
            x, tail[:, cols], [cw_ref[j:j + 1, cols] for j in range(LRU_CONV)])
        tail[:, cols] = x[ts - SUBLANES:, :]
        g = jnp.dot(xc.astype(BF16), wg_ref[n], preferred_element_type=F32)
        r = _sigmoid(g[:, :block] + brg_ref[:, cols])
        i = _sigmoid(g[:, block:] + big_ref[:, cols])
        lam = lam_ref[:, cols]
        log_sig = jnp.minimum(lam, 0.0) - jnp.log1p(jnp.exp(-jnp.abs(lam)))
        a = jnp.exp2(r * (log_sig * (LRU_C * LOG2_E)))
        a_s[:, cols] = a
        u_s[:, cols] = xc * i * jnp.sqrt(_one_minus_exp(r * (log_sig * (-2.0 * LRU_C)), a * a))

    def group(gi, h):
        base = pl.multiple_of(gi * SUBLANES, SUBLANES)
        for r8 in range(SUBLANES):
            h = a_s[pl.ds(base + r8, 1), :] * h + u_s[pl.ds(base + r8, 1), :]
            h_s[pl.ds(base + r8, 1), :] = h
        return h

    state[...] = lax.fori_loop(0, ts // SUBLANES, group, state[...])
    o_ref[...] = (h_s[...] * _silu(ag_ref[...].astype(F32))).astype(o_ref.dtype)


def _rglru(p, conv_w, conv_b, w_gates, b_rg, b_ig, lru_lambda, batch, seq, width, ts=256):
    n_blocks, block = w_gates.shape[0], w_gates.shape[1]
    ts = min(ts, seq)
    nt = seq // ts
    row = lambda b, t: b * nt + t
    vec = pl.BlockSpec((1, width), lambda b, t: (0, 0))
    return pl.pallas_call(
        functools.partial(_rglru_kernel, n_blocks=n_blocks, block=block),
        grid=(batch, nt),
        in_specs=[pl.BlockSpec((ts, width), lambda b, t: (row(b, t), 0)),
                  pl.BlockSpec((ts, width), lambda b, t: (row(b, t), 1)),
                  pl.BlockSpec((LRU_CONV, width), lambda b, t: (0, 0)),
                  vec,
                  pl.BlockSpec((n_blocks, block, 2 * block), lambda b, t: (0, 0, 0)),
                  vec, vec, vec],
        out_specs=pl.BlockSpec((ts, width), lambda b, t: (row(b, t), 0)),
        out_shape=jax.ShapeDtypeStruct((batch * seq, width), BF16),
        scratch_shapes=[pltpu.VMEM((SUBLANES, width), F32),
                        pltpu.VMEM((ts, width), F32),
                        pltpu.VMEM((ts, width), F32),
                        pltpu.VMEM((ts, width), F32),
                        pltpu.VMEM((1, width), F32)],
        compiler_params=_cparams(("parallel", "arbitrary")),
        name="rglru",
    )(p, p, conv_w, conv_b.reshape(1, width), w_gates, b_rg.reshape(1, width),
      b_ig.reshape(1, width), lru_lambda.reshape(1, width))


def _diff_attn_kernel(q_ref, k_ref, v_ref, g_ref, lam_ref, sw_ref, o_ref, m_s, l_s, acc_s):
    qi = pl.program_id(2)
    tq = q_ref.shape[0]
    n_blk = tq // LANES

    m_s[...] = jnp.full_like(m_s, MASK_VALUE)
    l_s[...] = jnp.zeros_like(l_s)
    acc_s[...] = jnp.zeros_like(acc_s)

    def chunk(j, diagonal):
        k0 = pl.multiple_of(j * tq, tq)
        v = v_ref[pl.ds(k0, tq), :]
        if diagonal:
            lower = (lax.broadcasted_iota(jnp.int32, (LANES, LANES), 1)
                     <= lax.broadcasted_iota(jnp.int32, (LANES, LANES), 0))
        for c in range(2):
            hd = slice(c * HEAD_DIM, (c + 1) * HEAD_DIM)
            s = lax.dot_general(q_ref[:, hd], k_ref[pl.ds(k0, tq), hd], (((1,), (1,)), ((), ())),
                                preferred_element_type=F32)
            m_prev = m_s[c]
            p_rows, l_rows, m_rows, alpha_rows = [], [], [], []
            for rb in range(n_blk):
                rows = slice(rb * LANES, (rb + 1) * LANES)
                live = range(rb + 1) if diagonal else range(n_blk)
                blocks = [s[rows, n * LANES:(n + 1) * LANES] for n in live]
                if diagonal:
                    blocks[rb] = jnp.where(lower, blocks[rb], MASK_VALUE)
                m_cur = blocks[0]
                for blk in blocks[1:]:
                    m_cur = jnp.maximum(m_cur, blk)
                m_new = jnp.maximum(m_prev[rows], jnp.max(m_cur, axis=-1, keepdims=True))
                alpha = jnp.exp2(m_prev[rows] - m_new)
                parts = [jnp.exp2(blk - m_new) for blk in blocks]
                l_new = alpha * l_s[c, rows, :]
                for part in parts:
                    l_new = l_new + part
                parts += [jnp.zeros((LANES, LANES), F32)] * (n_blk - len(parts))
                p_rows.append(jnp.concatenate(parts, axis=-1).astype(BF16))
                l_rows.append(l_new)
                m_rows.append(m_new)
                alpha_rows.append(alpha)
            l_s[c] = jnp.concatenate(l_rows, axis=0)
            m_s[c] = jnp.concatenate(m_rows, axis=0)
            alpha = jnp.concatenate(alpha_rows, axis=0)
            alpha_wide = jnp.concatenate([alpha] * (VALUE_DIM // LANES), axis=-1)
            acc_s[c] = alpha_wide * acc_s[c] + jnp.dot(jnp.concatenate(p_rows, axis=0), v,
                                                        preferred_element_type=F32)

    def below_diagonal(u, carry):
        chunk(2 * u, diagonal=False)
        chunk(2 * u + 1, diagonal=False)
        return carry

    lax.fori_loop(0, qi // 2, below_diagonal, 0)

    @pl.when(qi % 2 == 1)
    def _():
        chunk(qi - 1, diagonal=False)
        chunk(qi, diagonal=True)

    @pl.when(qi % 2 == 0)
    def _():
        chunk(qi, diagonal=True)

    lam_init = lam_ref[4:5, 0:1]
    lam = (jnp.exp(jnp.sum(lam_ref[0:1, :] * lam_ref[1:2, :], axis=-1, keepdims=True))
           - jnp.exp(jnp.sum(lam_ref[2:3, :] * lam_ref[3:4, :], axis=-1, keepdims=True))
           + lam_init)
    l0 = jnp.sum(l_s[0], axis=-1, keepdims=True)
    l1 = jnp.sum(l_s[1], axis=-1, keepdims=True)
    o = acc_s[0] / l0 - lam * (acc_s[1] / l1)
    o = o * lax.rsqrt(jnp.mean(o * o, axis=-1, keepdims=True) + SUBLN_EPS)
    o = o * sw_ref[...] * (1.0 - lam_init)
    o_ref[...] = (o * _silu(g_ref[...].astype(F32))).astype(o_ref.dtype)


def _diff_attention(qk, p, lam_rows, subln_w, batch, seq, heads, v_col, g_col, tq=512):
    tq = min(tq, seq)
    nq = seq // tq
    return pl.pallas_call(
        _diff_attn_kernel,
        grid=(batch, heads, nq),
        in_specs=[pl.BlockSpec((tq, VALUE_DIM), lambda b, h, i: (b * nq + i, h)),
                  pl.BlockSpec((seq, VALUE_DIM), lambda b, h, i: (b, heads + h)),
                  pl.BlockSpec((seq, VALUE_DIM), lambda b, h, i: (b, v_col + h)),
                  pl.BlockSpec((tq, VALUE_DIM), lambda b, h, i: (b * nq + i, g_col + h)),
                  pl.BlockSpec((SUBLANES, HEAD_DIM), lambda b, h, i: (0, 0)),
                  pl.BlockSpec((1, VALUE_DIM), lambda b, h, i: (0, 0))],
        out_specs=pl.BlockSpec((tq, VALUE_DIM), lambda b, h, i: (b * nq + i, h)),
        out_shape=jax.ShapeDtypeStruct((batch * seq, heads * VALUE_DIM), BF16),
        scratch_shapes=[pltpu.VMEM((2, tq, LANES), F32),
                        pltpu.VMEM((2, tq, LANES), F32),
                        pltpu.VMEM((2, tq, VALUE_DIM), F32)],
        compiler_params=_cparams(("parallel", "parallel", "arbitrary")),
        name="diff_attention",
    )(qk, qk, p, p, lam_rows, subln_w.reshape(1, VALUE_DIM))


def _sconv_kernel(b_ref, c_ref, x_ref, g_ref, w_ref, o_ref, tail):
    ts = b_ref.shape[0]

    @pl.when(pl.program_id(1) == 0)
    def _():
        tail[...] = jnp.zeros_like(tail)

    z = c_ref[...].astype(F32) * x_ref[...].astype(F32)
    conv = _causal_taps(z, tail[...], [w_ref[j:j + 1, :] for j in range(SCONV_K)])
    tail[...] = z[ts - SUBLANES:, :]
    o_ref[...] = (b_ref[...].astype(F32) * conv * _silu(g_ref[...].astype(F32))).astype(o_ref.dtype)


def _short_conv(p, conv_w, batch, seq, width, first_col, ts=256):
    ts = min(ts, seq)
    nt = seq // ts
    spec = lambda c: pl.BlockSpec((ts, width), lambda b, t: (b * nt + t, first_col + c))
    return pl.pallas_call(
        _sconv_kernel,
        grid=(batch, nt),
        in_specs=[spec(0), spec(1), spec(2), spec(3),
                  pl.BlockSpec((SCONV_K, width), lambda b, t: (0, 0))],
        out_specs=pl.BlockSpec((ts, width), lambda b, t: (b * nt + t, 0)),
        out_shape=jax.ShapeDtypeStruct((batch * seq, width), BF16),
        scratch_shapes=[pltpu.VMEM((SUBLANES, width), F32)],
        compiler_params=_cparams(("parallel", "arbitrary")),
        name="short_conv",
    )(p, p, p, p, conv_w)


def _merge_kernel(ya_ref, yb_ref, yc_ref, wa_ref, wb_ref, wc_ref, ga_ref, gb_ref, gc_ref,
                  ba_ref, bb_ref, bc_ref, o_ref):
    out = None
    for y_ref, w_ref, gl_ref, b_ref in ((ya_ref, wa_ref, ga_ref, ba_ref),
                                        (yb_ref, wb_ref, gb_ref, bb_ref),
                                        (yc_ref, wc_ref, gc_ref, bc_ref)):
        gate = _sigmoid(gl_ref[...].astype(F32) + b_ref[...])
        term = gate * jnp.dot(y_ref[...], w_ref[...], preferred_element_type=F32)
        out = term if out is None else out + term
    o_ref[...] = out.astype(o_ref.dtype)


def _merge(ya, yb, yc, w_branch, p, gate_b, gl_col, tm=1024, tn=512):
    m, width = ya.shape
    d = w_branch.shape[1]
    tm, tn = min(tm, m), min(tn, d)
    y_spec = pl.BlockSpec((tm, width), lambda i, j: (i, 0))
    w_spec = lambda br: pl.BlockSpec((width, tn), lambda i, j: (br, j))
    g_spec = lambda br: pl.BlockSpec((tm, tn), lambda i, j: (i, (gl_col + br * d) // tn + j))
    b_spec = lambda br: pl.BlockSpec((None, 1, tn), lambda i, j: (br, 0, j))
    gate_b3 = gate_b.reshape(N_BRANCHES, 1, d)
    return pl.pallas_call(
        _merge_kernel,
        grid=(m // tm, d // tn),
        in_specs=[y_spec, y_spec, y_spec, w_spec(0), w_spec(1), w_spec(2),
                  g_spec(0), g_spec(1), g_spec(2), b_spec(0), b_spec(1), b_spec(2)],
        out_specs=pl.BlockSpec((tm, tn), lambda i, j: (i, j)),
        out_shape=jax.ShapeDtypeStruct((m, d), BF16),
        compiler_params=_cparams(("parallel", "parallel")),
        name="gated_merge",
    )(ya, yb, yc, w_branch, w_branch, w_branch, p, p, p, gate_b3, gate_b3, gate_b3)


def kernel(x, positions, norm_w, w_in, gate_b, conv_a_w, conv_a_b, w_rg, b_rg, w_ig, b_ig,
           lru_lambda, lam_q1, lam_k1, lam_q2, lam_k2, subln_w, conv_c_w, w_branch, w_out,
           final_norm_w):
    batch, seq, d = x.shape
    depth = norm_w.shape[0]
    bw = w_branch.shape[2]
    heads = bw // VALUE_DIM
    m = batch * seq
    assert w_in.shape[2] == N_BRANCH_SLICES * bw + N_BRANCHES * d
    assert seq % SUBLANES == 0 and bw % VALUE_DIM == 0

    w_branch_rows = w_branch.reshape(depth, N_BRANCHES * bw, d)
    w_in_l = w_in[0].astype(BF16)
    w_branch_l = w_branch_rows[0].astype(BF16)
    w_out_l = w_out[0].astype(BF16)
    w_gates_bf = jnp.concatenate([w_rg, w_ig], axis=-1).astype(BF16)

    cos, sin = _rope_tables(positions)
    xf = x.reshape(m, d)
    for l in range(depth):
        lam_init = 0.8 - 0.6 * math.exp(-0.3 * l)
        lam_rows = jnp.concatenate(
            [lam_q1[l][None], lam_k1[l][None], lam_q2[l][None], lam_k2[l][None],
             jnp.full((1, HEAD_DIM), lam_init, F32), jnp.zeros((SUBLANES - 5, HEAD_DIM), F32)], axis=0)

        h = _rmsnorm(xf, norm_w[l], BF16)
        last = l == depth - 1
        p, next_w_in = _in_proj(h, w_in_l, skip_start=2 * bw, skip_cols=2 * bw,
                                cast_srcs=[] if last else [w_in], cast_layer=l + 1)
        qk = _qk_proj_rope(h, w_in_l, col_start=2 * bw, q_cols=bw, cos=cos, sin=sin)

        ya = _rglru(p, conv_a_w[l], conv_a_b[l], w_gates_bf[l], b_rg[l], b_ig[l], lru_lambda[l],
                    batch, seq, bw)
        yb = _diff_attention(qk, p, lam_rows, subln_w[l], batch, seq, heads,
                             v_col=2 * heads, g_col=3 * heads)
        yc = _short_conv(p, conv_c_w[l], batch, seq, bw, first_col=4)
        merged = _merge(ya, yb, yc, w_branch_l, p, gate_b[l], gl_col=8 * bw)
        xf, next_w = _out_proj_residual(merged, w_out_l, xf,
                                        cast_srcs=[] if last else [w_branch_rows, w_out],
                                        cast_layer=l + 1)
        if not last:
            (w_in_l,), (w_branch_l, w_out_l) = next_w_in, next_w
    return _rmsnorm(xf, final_norm_w, F32).reshape(batch, seq, d)
```

```python
import functools
import math
from typing import NamedTuple

import jax
import jax.numpy as jnp
from jax import lax
from jax.experimental import pallas as pl
from jax.experimental.pallas import tpu as pltpu

F32 = jnp.float32
BF16 = jnp.bfloat16

HEAD_DIM = 128
VALUE_DIM = 2 * HEAD_DIM
LRU_C = 8.0
LRU_CONV = 4
SCONV_K = 3
ROPE_THETA = 10000.0
NORM_EPS = 1e-6
SUBLN_EPS = 1e-5
N_BRANCHES = 3
N_BRANCH_SLICES = 10

V7X_VMEM_BYTES = 64 * 1024 * 1024
VMEM_LIMIT = V7X_VMEM_BYTES - 8 * 1024 * 1024
SUBLANES = 8
LANES = 128
MASK_VALUE = -1e30
LOG2_E = 1.4426950408889634
CAST_TILE_W_IN = (512, 1024)
CAST_TILE_SMALL = (256, 1024)


def _cparams(semantics):
    return pltpu.CompilerParams(dimension_semantics=semantics, vmem_limit_bytes=VMEM_LIMIT)


def _sigmoid(x):
    return 1.0 / (1.0 + jnp.exp(-x))


def _silu(x):
    return x * _sigmoid(x)


def _one_minus_exp(z, exp_neg_z):
    poly = 1.0 - (z * 0.5) * (1.0 - z * (1.0 / 3.0))
    return jnp.where(z < 1.0 / 64.0, z * poly, 1.0 - exp_neg_z)


def _causal_taps(x, prev_tail, conv_w_rows):
    k_width = len(conv_w_rows)
    out = conv_w_rows[k_width - 1] * x
    head_rows = lax.broadcasted_iota(jnp.int32, (SUBLANES, x.shape[1]), 0)
    for shift in range(1, k_width):
        rolled = pltpu.roll(x, shift, axis=0)
        head = jnp.where(head_rows < shift, pltpu.roll(prev_tail, shift, axis=0), rolled[0:SUBLANES])
        shifted = jnp.concatenate([head, rolled[SUBLANES:]], axis=0)
        out = out + conv_w_rows[k_width - 1 - shift] * shifted
    return out


def _rmsnorm_kernel(x_ref, w_ref, o_ref, *, eps):
    x = x_ref[...].astype(F32)
    ms = jnp.mean(x * x, axis=-1, keepdims=True)
    o_ref[...] = (x * lax.rsqrt(ms + eps) * w_ref[...]).astype(o_ref.dtype)


def _rmsnorm(x, w, out_dtype, eps=NORM_EPS):
    m, d = x.shape
    tr = min(512, m)
    return pl.pallas_call(
        functools.partial(_rmsnorm_kernel, eps=eps),
        grid=(m // tr,),
        in_specs=[pl.BlockSpec((tr, d), lambda i: (i, 0)),
                  pl.BlockSpec((1, d), lambda i: (0, 0))],
        out_specs=pl.BlockSpec((tr, d), lambda i: (i, 0)),
        out_shape=jax.ShapeDtypeStruct((m, d), out_dtype),
        compiler_params=_cparams(("parallel",)),
        name="rmsnorm",
    )(x, w.reshape(1, d))


def _rope_table_kernel(pos_ref, invf_ref, sign_ref, cos_ref, sin_ref):
    ang = pos_ref[...].astype(F32) * invf_ref[...]
    cos_ref[...] = jnp.cos(ang)
    sin_ref[...] = jnp.sin(ang) * sign_ref[...]


def _rope_tables(positions):
    m = positions.size
    half = HEAD_DIM // 2
    inv_freq = ROPE_THETA ** (-jnp.arange(0, half, dtype=F32) * (2.0 / HEAD_DIM))
    invf = jnp.concatenate([inv_freq, inv_freq]).reshape(1, HEAD_DIM)
    sign = jnp.concatenate([-jnp.ones((half,), F32), jnp.ones((half,), F32)]).reshape(1, HEAD_DIM)
    tr = min(512, m)
    return pl.pallas_call(
        _rope_table_kernel,
        grid=(m // tr,),
        in_specs=[pl.BlockSpec((tr, 1), lambda i: (i, 0)),
                  pl.BlockSpec((1, HEAD_DIM), lambda i: (0, 0)),
                  pl.BlockSpec((1, HEAD_DIM), lambda i: (0, 0))],
        out_specs=[pl.BlockSpec((tr, HEAD_DIM), lambda i: (i, 0)),
                   pl.BlockSpec((tr, HEAD_DIM), lambda i: (i, 0))],
        out_shape=[jax.ShapeDtypeStruct((m, HEAD_DIM), F32)] * 2,
        compiler_params=_cparams(("parallel",)),
        name="rope_tables",
    )(positions.reshape(m, 1), invf, sign)


class _CastJob(NamedTuple):
    src: jax.Array
    layer: int
    first_step: int
    tile: tuple
    n_col_tiles: int
    n_tiles: int


def _plan_cast_jobs(srcs, layer, n_steps):
    jobs, rest, step = [], [], 0
    for src, tile in srcs:
        _, rows, cols = src.shape
        tr, tc = min(tile[0], rows), min(tile[1], cols)
        n_tiles = (rows // tr) * (cols // tc)
        if rows % tr or cols % tc or step + n_tiles > n_steps:
            rest.append(src)
            continue
        jobs.append(_CastJob(src, layer, step, (tr, tc), cols // tc, n_tiles))
        step += n_tiles
    return jobs, rest


def _cast_job_specs(job, n_j):
    def tile_index(i, j):
        t = jnp.clip(i * n_j + j - job.first_step, 0, job.n_tiles - 1)
        return t // job.n_col_tiles, t % job.n_col_tiles
    in_spec = pl.BlockSpec((None,) + job.tile, lambda i, j: (job.layer,) + tile_index(i, j))
    out_spec = pl.BlockSpec(job.tile, tile_index)
    return in_spec, out_spec, jax.ShapeDtypeStruct(job.src.shape[1:], BF16)


def _run_cast_jobs(jobs, src_refs, dst_refs):
    step = pl.program_id(0) * pl.num_programs(1) + pl.program_id(1)
    for job, src_ref, dst_ref in zip(jobs, src_refs, dst_refs):
        @pl.when((step >= job.first_step) & (step < job.first_step + job.n_tiles))
        def _():
            dst_ref[...] = src_ref[...].astype(BF16)


def _in_proj_kernel(x_ref, w_ref, *refs, jobs):
    o_ref = refs[len(jobs)]
    o_ref[...] = jnp.dot(x_ref[...], w_ref[...], preferred_element_type=F32).astype(o_ref.dtype)
    _run_cast_jobs(jobs, refs[:len(jobs)], refs[len(jobs) + 1:])


def _in_proj(x, w, skip_start, skip_cols, cast_srcs, cast_layer, tm=1024, tn=1024):
    m, k = x.shape
    n = w.shape[1] - skip_cols
    tm, tn = min(tm, m), min(tn, n, skip_cols)
    first, skip = skip_start // tn, skip_cols // tn
    grid = (m // tm, n // tn)
    jobs, rest = _plan_cast_jobs(cast_srcs, cast_layer, grid[0] * grid[1])
    specs = [_cast_job_specs(job, grid[1]) for job in jobs]
    outs = pl.pallas_call(
        functools.partial(_in_proj_kernel, jobs=jobs),
        grid=grid,
        in_specs=[pl.BlockSpec((tm, k), lambda i, j: (i, 0)),
                  pl.BlockSpec((k, tn), lambda i, j: (0, jnp.where(j >= first, j + skip, j)))]
                 + [s[0] for s in specs],
        out_specs=[pl.BlockSpec((tm, tn), lambda i, j: (i, j))] + [s[1] for s in specs],
        out_shape=[jax.ShapeDtypeStruct((m, n), BF16)] + [s[2] for s in specs],
        compiler_params=_cparams(("arbitrary", "arbitrary")),
        name="in_proj",
    )(x, w, *[job.src for job in jobs])
    by_src = {id(job.src): out for job, out in zip(jobs, outs[1:])}
    by_src.update({id(src): src[cast_layer].astype(BF16) for src in rest})
    return outs[0], [by_src[id(src)] for src, _ in cast_srcs]


def _matmul_rope_kernel(x_ref, w_ref, cos_ref, sin_ref, o_ref, *, q_col_blocks, q_scale):
    acc = jnp.dot(x_ref[...], w_ref[...], preferred_element_type=F32)
    scale = jnp.where(pl.program_id(1) < q_col_blocks, q_scale, 1.0).astype(F32)
    cos = cos_ref[...] * scale
    sin = sin_ref[...] * scale
    for c in range(acc.shape[1] // HEAD_DIM):
        t = acc[:, c * HEAD_DIM:(c + 1) * HEAD_DIM]
        swapped = pltpu.roll(t, HEAD_DIM // 2, axis=1)
        o_ref[:, c * HEAD_DIM:(c + 1) * HEAD_DIM] = (t * cos + swapped * sin).astype(o_ref.dtype)


def _qk_proj_rope(x, w, col_start, q_cols, cos, sin, tm=1024, tn=1024):
    m, k = x.shape
    n = 2 * q_cols
    tm, tn = min(tm, m), min(tn, q_cols)
    first = col_start // tn
    return pl.pallas_call(
        functools.partial(_matmul_rope_kernel, q_col_blocks=q_cols // tn,
                          q_scale=HEAD_DIM ** -0.5 * LOG2_E),
        grid=(m // tm, n // tn),
        in_specs=[pl.BlockSpec((tm, k), lambda i, j: (i, 0)),
                  pl.BlockSpec((k, tn), lambda i, j: (0, first + j)),
                  pl.BlockSpec((tm, HEAD_DIM), lambda i, j: (i, 0)),
                  pl.BlockSpec((tm, HEAD_DIM), lambda i, j: (i, 0))],
        out_specs=pl.BlockSpec((tm, tn), lambda i, j: (i, j)),
        out_shape=jax.ShapeDtypeStruct((m, n), BF16),
        compiler_params=_cparams(("parallel", "parallel")),
        name="qk_proj_rope",
    )(x, w, cos, sin)


def _matmul_residual_kernel(a_ref, w_ref, r_ref, o_ref):
    o_ref[...] = r_ref[...] + jnp.dot(a_ref[...], w_ref[...], preferred_element_type=F32)


def _out_proj_residual(a, w, resid, tm=1024, tn=1024):
    m, k = a.shape
    n = w.shape[1]
    tm, tn = min(tm, m), min(tn, n)
    return pl.pallas_call(
        _matmul_residual_kernel,
        grid=(m // tm, n // tn),
        in_specs=[pl.BlockSpec((tm, k), lambda i, j: (i, 0)),
                  pl.BlockSpec((k, tn), lambda i, j: (0, j)),
                  pl.BlockSpec((tm, tn), lambda i, j: (i, j))],
        out_specs=pl.BlockSpec((tm, tn), lambda i, j: (i, j)),
        out_shape=jax.ShapeDtypeStruct((m, n), F32),
        compiler_params=_cparams(("parallel", "parallel")),
        name="out_proj_residual",
    )(a, w, resid)


def _rglru_kernel(ax_ref, ag_ref, cw_ref, cb_ref, wg_ref, brg_ref, big_ref, lam_ref, o_ref,
                  tail, a_s, u_s, h_s, state, *, n_blocks, block):
    ts = ax_ref.shape[0]

    @pl.when(pl.program_id(1) == 0)
    def _():
        tail[...] = jnp.zeros_like(tail)
        state[...] = jnp.zeros_like(state)

    for n in range(n_blocks):
        cols = slice(n * block, (n + 1) * block)
        x = ax_ref[:, cols].astype(F32)
        xc = cb_ref[:, cols] + _causal_taps(
            x, tail[:, cols], [cw_ref[j:j + 1, cols] for j in range(LRU_CONV)])
        tail[:, cols] = x[ts - SUBLANES:, :]
        g = jnp.dot(xc.astype(BF16), wg_ref[n], preferred_element_type=F32)
        r = _sigmoid(g[:, :block] + brg_ref[:, cols])
        i = _sigmoid(g[:, block:] + big_ref[:, cols])
        lam = lam_ref[:, cols]
        log_sig = jnp.minimum(lam, 0.0) - jnp.log1p(jnp.exp(-jnp.abs(lam)))
        a = jnp.exp2(r * (log_sig * (LRU_C * LOG2_E)))
        a_s[:, cols] = a
        u_s[:, cols] = xc * i * jnp.sqrt(_one_minus_exp(r * (log_sig * (-2.0 * LRU_C)), a * a))

    def group(gi, h):
        base = pl.multiple_of(gi * SUBLANES, SUBLANES)
        for r8 in range(SUBLANES):
            h = a_s[pl.ds(base + r8, 1), :] * h + u_s[pl.ds(base + r8, 1), :]
            h_s[pl.ds(base + r8, 1), :] = h
        return h

    state[...] = lax.fori_loop(0, ts // SUBLANES, group, state[...])
    o_ref[...] = (h_s[...] * _silu(ag_ref[...].astype(F32))).astype(o_ref.dtype)


def _rglru(p, conv_w, conv_b, w_gates, b_rg, b_ig, lru_lambda, batch, seq, width, ts=256):
    n_blocks, block = w_gates.shape[0], w_gates.shape[1]
    ts = min(ts, seq)
    nt = seq // ts
    row = lambda b, t: b * nt + t
    vec = pl.BlockSpec((1, width), lambda b, t: (0, 0))
    return pl.pallas_call(
        functools.partial(_rglru_kernel, n_blocks=n_blocks, block=block),
        grid=(batch, nt),
        in_specs=[pl.BlockSpec((ts, width), lambda b, t: (row(b, t), 0)),
                  pl.BlockSpec((ts, width), lambda b, t: (row(b, t), 1)),
                  pl.BlockSpec((LRU_CONV, width), lambda b, t: (0, 0)),
                  vec,
                  pl.BlockSpec((n_blocks, block, 2 * block), lambda b, t: (0, 0, 0)),
                  vec, vec, vec],
        out_specs=pl.BlockSpec((ts, width), lambda b, t: (row(b, t), 0)),
        out_shape=jax.ShapeDtypeStruct((batch * seq, width), BF16),
        scratch_shapes=[pltpu.VMEM((SUBLANES, width), F32),
                        pltpu.VMEM((ts, width), F32),
                        pltpu.VMEM((ts, width), F32),
                        pltpu.VMEM((ts, width), F32),
                        pltpu.VMEM((1, width), F32)],
        compiler_params=_cparams(("parallel", "arbitrary")),
        name="rglru",
    )(p, p, conv_w, conv_b.reshape(1, width), w_gates, b_rg.reshape(1, width),
      b_ig.reshape(1, width), lru_lambda.reshape(1, width))


def _diff_attn_kernel(q_ref, k_ref, v_ref, g_ref, lam_ref, sw_ref, o_ref, m_s, l_s, acc_s):
    qi = pl.program_id(2)
    tq = q_ref.shape[0]
    n_blk = tq // LANES

    m_s[...] = jnp.full_like(m_s, MASK_VALUE)
    l_s[...] = jnp.zeros_like(l_s)
    acc_s[...] = jnp.zeros_like(acc_s)

    def chunk(j, diagonal):
        k0 = pl.multiple_of(j * tq, tq)
        v = v_ref[pl.ds(k0, tq), :]
        if diagonal:
            lower = (lax.broadcasted_iota(jnp.int32, (LANES, LANES), 1)
                     <= lax.broadcasted_iota(jnp.int32, (LANES, LANES), 0))
        for c in range(2):
            hd = slice(c * HEAD_DIM, (c + 1) * HEAD_DIM)
            s = lax.dot_general(q_ref[:, hd], k_ref[pl.ds(k0, tq), hd], (((1,), (1,)), ((), ())),
                                preferred_element_type=F32)
            m_prev = m_s[c]
            p_rows, l_rows, m_rows, alpha_rows = [], [], [], []
            for rb in range(n_blk):
                rows = slice(rb * LANES, (rb + 1) * LANES)
                live = range(rb + 1) if diagonal else range(n_blk)
                blocks = [s[rows, n * LANES:(n + 1) * LANES] for n in live]
                if diagonal:
                    blocks[rb] = jnp.where(lower, blocks[rb], MASK_VALUE)
                m_cur = blocks[0]
                for blk in blocks[1:]:
                    m_cur = jnp.maximum(m_cur, blk)
                m_new = jnp.maximum(m_prev[rows], jnp.max(m_cur, axis=-1, keepdims=True))
                alpha = jnp.exp2(m_prev[rows] - m_new)
                parts = [jnp.exp2(blk - m_new) for blk in blocks]
                l_new = alpha * l_s[c, rows, :]
                for part in parts:
                    l_new = l_new + part
                parts += [jnp.zeros((LANES, LANES), F32)] * (n_blk - len(parts))
                p_rows.append(jnp.concatenate(parts, axis=-1).astype(BF16))
                l_rows.append(l_new)
                m_rows.append(m_new)
                alpha_rows.append(alpha)
            l_s[c] = jnp.concatenate(l_rows, axis=0)
            m_s[c] = jnp.concatenate(m_rows, axis=0)
            alpha = jnp.concatenate(alpha_rows, axis=0)
            alpha_wide = jnp.concatenate([alpha] * (VALUE_DIM // LANES), axis=-1)
            acc_s[c] = alpha_wide * acc_s[c] + jnp.dot(jnp.concatenate(p_rows, axis=0), v,
                                                        preferred_element_type=F32)

    def below_diagonal(u, carry):
        chunk(2 * u, diagonal=False)
        chunk(2 * u + 1, diagonal=False)
        return carry

    lax.fori_loop(0, qi // 2, below_diagonal, 0)

    @pl.when(qi % 2 == 1)
    def _():
        chunk(qi - 1, diagonal=False)
        chunk(qi, diagonal=True)

    @pl.when(qi % 2 == 0)
    def _():
        chunk(qi, diagonal=True)

    lam_init = lam_ref[4:5, 0:1]
    lam = (jnp.exp(jnp.sum(lam_ref[0:1, :] * lam_ref[1:2, :], axis=-1, keepdims=True))
           - jnp.exp(jnp.sum(lam_ref[2:3, :] * lam_ref[3:4, :], axis=-1, keepdims=True))
           + lam_init)
    l0 = jnp.sum(l_s[0], axis=-1, keepdims=True)
    l1 = jnp.sum(l_s[1], axis=-1, keepdims=True)
    o = acc_s[0] / l0 - lam * (acc_s[1] / l1)
    o = o * lax.rsqrt(jnp.mean(o * o, axis=-1, keepdims=True) + SUBLN_EPS)
    o = o * sw_ref[...] * (1.0 - lam_init)
    o_ref[...] = (o * _silu(g_ref[...].astype(F32))).astype(o_ref.dtype)


def _diff_attention(qk, p, lam_rows, subln_w, batch, seq, heads, v_col, g_col, tq=512):
    tq = min(tq, seq)
    nq = seq // tq
    return pl.pallas_call(
        _diff_attn_kernel,
        grid=(batch, heads, nq),
        in_specs=[pl.BlockSpec((tq, VALUE_DIM), lambda b, h, i: (b * nq + i, h)),
                  pl.BlockSpec((seq, VALUE_DIM), lambda b, h, i: (b, heads + h)),
                  pl.BlockSpec((seq, VALUE_DIM), lambda b, h, i: (b, v_col + h)),
                  pl.BlockSpec((tq, VALUE_DIM), lambda b, h, i: (b * nq + i, g_col + h)),
                  pl.BlockSpec((SUBLANES, HEAD_DIM), lambda b, h, i: (0, 0)),
                  pl.BlockSpec((1, VALUE_DIM), lambda b, h, i: (0, 0))],
        out_specs=pl.BlockSpec((tq, VALUE_DIM), lambda b, h, i: (b * nq + i, h)),
        out_shape=jax.ShapeDtypeStruct((batch * seq, heads * VALUE_DIM), BF16),
        scratch_shapes=[pltpu.VMEM((2, tq, LANES), F32),
                        pltpu.VMEM((2, tq, LANES), F32),
                        pltpu.VMEM((2, tq, VALUE_DIM), F32)],
        compiler_params=_cparams(("parallel", "parallel", "arbitrary")),
        name="diff_attention",
    )(qk, qk, p, p, lam_rows, subln_w.reshape(1, VALUE_DIM))


def _sconv_kernel(b_ref, c_ref, x_ref, g_ref, w_ref, o_ref, tail):
    ts = b_ref.shape[0]

    @pl.when(pl.program_id(1) == 0)
    def _():
        tail[...] = jnp.zeros_like(tail)

    z = c_ref[...].astype(F32) * x_ref[...].astype(F32)
    conv = _causal_taps(z, tail[...], [w_ref[j:j + 1, :] for j in range(SCONV_K)])
    tail[...] = z[ts - SUBLANES:, :]
    o_ref[...] = (b_ref[...].astype(F32) * conv * _silu(g_ref[...].astype(F32))).astype(o_ref.dtype)


def _short_conv(p, conv_w, batch, seq, width, first_col, ts=256):
    ts = min(ts, seq)
    nt = seq // ts
    spec = lambda c: pl.BlockSpec((ts, width), lambda b, t: (b * nt + t, first_col + c))
    return pl.pallas_call(
        _sconv_kernel,
        grid=(batch, nt),
        in_specs=[spec(0), spec(1), spec(2), spec(3),
                  pl.BlockSpec((SCONV_K, width), lambda b, t: (0, 0))],
        out_specs=pl.BlockSpec((ts, width), lambda b, t: (b * nt + t, 0)),
        out_shape=jax.ShapeDtypeStruct((batch * seq, width), BF16),
        scratch_shapes=[pltpu.VMEM((SUBLANES, width), F32)],
        compiler_params=_cparams(("parallel", "arbitrary")),
        name="short_conv",
    )(p, p, p, p, conv_w)


def _merge_kernel(ya_ref, yb_ref, yc_ref, wa_ref, wb_ref, wc_ref, ga_ref, gb_ref, gc_ref,
                  ba_ref, bb_ref, bc_ref, o_ref):
    out = None
    for y_ref, w_ref, gl_ref, b_ref in ((ya_ref, wa_ref, ga_ref, ba_ref),
                                        (yb_ref, wb_ref, gb_ref, bb_ref),
                                        (yc_ref, wc_ref, gc_ref, bc_ref)):
        gate = _sigmoid(gl_ref[...].astype(F32) + b_ref[...])
        term = gate * jnp.dot(y_ref[...], w_ref[...], preferred_element_type=F32)
        out = term if out is None else out + term
    o_ref[...] = out.astype(o_ref.dtype)


def _merge(ya, yb, yc, w_branch, p, gate_b, gl_col, tm=1024, tn=512):
    m, width = ya.shape
    d = w_branch.shape[1]
    tm, tn = min(tm, m), min(tn, d)
    y_spec = pl.BlockSpec((tm, width), lambda i, j: (i, 0))
    w_spec = lambda br: pl.BlockSpec((width, tn), lambda i, j: (br, j))
    g_spec = lambda br: pl.BlockSpec((tm, tn), lambda i, j: (i, (gl_col + br * d) // tn + j))
    b_spec = lambda br: pl.BlockSpec((None, 1, tn), lambda i, j: (br, 0, j))
    gate_b3 = gate_b.reshape(N_BRANCHES, 1, d)
    return pl.pallas_call(
        _merge_kernel,
        grid=(m // tm, d // tn),
        in_specs=[y_spec, y_spec, y_spec, w_spec(0), w_spec(1), w_spec(2),
                  g_spec(0), g_spec(1), g_spec(2), b_spec(0), b_spec(1), b_spec(2)],
        out_specs=pl.BlockSpec((tm, tn), lambda i, j: (i, j)),
        out_shape=jax.ShapeDtypeStruct((m, d), BF16),
        compiler_params=_cparams(("parallel", "parallel")),
        name="gated_merge",
    )(ya, yb, yc, w_branch, w_branch, w_branch, p, p, p, gate_b3, gate_b3, gate_b3)


def kernel(x, positions, norm_w, w_in, gate_b, conv_a_w, conv_a_b, w_rg, b_rg, w_ig, b_ig,
           lru_lambda, lam_q1, lam_k1, lam_q2, lam_k2, subln_w, conv_c_w, w_branch, w_out,
           final_norm_w):
    batch, seq, d = x.shape
    depth = norm_w.shape[0]
    bw = w_branch.shape[2]
    heads = bw // VALUE_DIM
    m = batch * seq
    assert w_in.shape[2] == N_BRANCH_SLICES * bw + N_BRANCHES * d
    assert seq % SUBLANES == 0 and bw % VALUE_DIM == 0

    w_branch_rows = w_branch.reshape(depth, N_BRANCHES * bw, d)
    w_in_l = w_in[0].astype(BF16)
    w_branch_l = w_branch_rows[0].astype(BF16)
    w_out_l = w_out[0].astype(BF16)
    w_gates_bf = jnp.concatenate([w_rg, w_ig], axis=-1).astype(BF16)

    cos, sin = _rope_tables(positions)
    xf = x.reshape(m, d)
    for l in range(depth):
        lam_init = 0.8 - 0.6 * math.exp(-0.3 * l)
        lam_rows = jnp.concatenate(
            [lam_q1[l][None], lam_k1[l][None], lam_q2[l][None], lam_k2[l][None],
             jnp.full((1, HEAD_DIM), lam_init, F32), jnp.zeros((SUBLANES - 5, HEAD_DIM), F32)], axis=0)

        h = _rmsnorm(xf, norm_w[l], BF16)
        last = l == depth - 1
        next_srcs = [] if last else [(w_in, CAST_TILE_W_IN), (w_branch_rows, CAST_TILE_SMALL),
                                     (w_out, CAST_TILE_SMALL)]
        p, next_w = _in_proj(h, w_in_l, skip_start=2 * bw, skip_cols=2 * bw,
                             cast_srcs=next_srcs, cast_layer=l + 1)
        qk = _qk_proj_rope(h, w_in_l, col_start=2 * bw, q_cols=bw, cos=cos, sin=sin)

        ya = _rglru(p, conv_a_w[l], conv_a_b[l], w_gates_bf[l], b_rg[l], b_ig[l], lru_lambda[l],
                    batch, seq, bw)
        yb = _diff_attention(qk, p, lam_rows, subln_w[l], batch, seq, heads,
                             v_col=2 * heads, g_col=3 * heads)
        yc = _short_conv(p, conv_c_w[l], batch, seq, bw, first_col=4)
        merged = _merge(ya, yb, yc, w_branch_l, p, gate_b[l], gl_col=8 * bw)
        xf = _out_proj_residual(merged, w_out_l, xf)
        if not last:
            w_in_l, w_branch_l, w_out_l = next_w
    return _rmsnorm(xf, final_norm_w, F32).reshape(batch, seq, d)
```

```python
import functools
import math
from typing import NamedTuple

import jax
import jax.numpy as jnp
from jax import lax
from jax.experimental import pallas as pl
from jax.experimental.pallas import tpu as pltpu

F32 = jnp.float32
BF16 = jnp.bfloat16

HEAD_DIM = 128
VALUE_DIM = 2 * HEAD_DIM
LRU_C = 8.0
LRU_CONV = 4
SCONV_K = 3
ROPE_THETA = 10000.0
NORM_EPS = 1e-6
SUBLN_EPS = 1e-5
N_BRANCHES = 3
N_BRANCH_SLICES = 10

V7X_VMEM_BYTES = 64 * 1024 * 1024
VMEM_LIMIT = V7X_VMEM_BYTES - 8 * 1024 * 1024
SUBLANES = 8
LANES = 128
MASK_VALUE = -1e30
LOG2_E = 1.4426950408889634
GROUP = 4
CAST_TILE_W_IN = (512, 1024)
CAST_TILE_SMALL = (256, 1024)


def _cparams(semantics):
    return pltpu.CompilerParams(dimension_semantics=semantics, vmem_limit_bytes=VMEM_LIMIT)


def _sigmoid(x):
    return 1.0 / (1.0 + jnp.exp(-x))


def _silu(x):
    return x * _sigmoid(x)


def _one_minus_exp(z, exp_neg_z):
    poly = 1.0 - (z * 0.5) * (1.0 - z * (1.0 / 3.0))
    return jnp.where(z < 1.0 / 64.0, z * poly, 1.0 - exp_neg_z)


def _causal_taps(x, prev_tail, conv_w_rows):
    k_width = len(conv_w_rows)
    out = conv_w_rows[k_width - 1] * x
    head_rows = lax.broadcasted_iota(jnp.int32, (SUBLANES, x.shape[1]), 0)
    for shift in range(1, k_width):
        rolled = pltpu.roll(x, shift, axis=0)
        head = jnp.where(head_rows < shift, pltpu.roll(prev_tail, shift, axis=0), rolled[0:SUBLANES])
        shifted = jnp.concatenate([head, rolled[SUBLANES:]], axis=0)
        out = out + conv_w_rows[k_width - 1 - shift] * shifted
    return out


def _rmsnorm_kernel(x_ref, w_ref, o_ref, *, eps):
    x = x_ref[...].astype(F32)
    ms = jnp.mean(x * x, axis=-1, keepdims=True)
    o_ref[...] = (x * lax.rsqrt(ms + eps) * w_ref[...]).astype(o_ref.dtype)


def _rmsnorm(x, w, out_dtype, eps=NORM_EPS):
    m, d = x.shape
    tr = min(512, m)
    return pl.pallas_call(
        functools.partial(_rmsnorm_kernel, eps=eps),
        grid=(m // tr,),
        in_specs=[pl.BlockSpec((tr, d), lambda i: (i, 0)),
                  pl.BlockSpec((1, d), lambda i: (0, 0))],
        out_specs=pl.BlockSpec((tr, d), lambda i: (i, 0)),
        out_shape=jax.ShapeDtypeStruct((m, d), out_dtype),
        compiler_params=_cparams(("parallel",)),
        name="rmsnorm",
    )(x, w.reshape(1, d))


def _rope_table_kernel(pos_ref, invf_ref, sign_ref, cos_ref, sin_ref):
    ang = pos_ref[...].astype(F32) * invf_ref[...]
    cos_ref[...] = jnp.cos(ang)
    sin_ref[...] = jnp.sin(ang) * sign_ref[...]


def _rope_tables(positions):
    m = positions.size
    half = HEAD_DIM // 2
    inv_freq = ROPE_THETA ** (-jnp.arange(0, half, dtype=F32) * (2.0 / HEAD_DIM))
    invf = jnp.concatenate([inv_freq, inv_freq]).reshape(1, HEAD_DIM)
    sign = jnp.concatenate([-jnp.ones((half,), F32), jnp.ones((half,), F32)]).reshape(1, HEAD_DIM)
    tr = min(512, m)
    return pl.pallas_call(
        _rope_table_kernel,
        grid=(m // tr,),
        in_specs=[pl.BlockSpec((tr, 1), lambda i: (i, 0)),
                  pl.BlockSpec((1, HEAD_DIM), lambda i: (0, 0)),
                  pl.BlockSpec((1, HEAD_DIM), lambda i: (0, 0))],
        out_specs=[pl.BlockSpec((tr, HEAD_DIM), lambda i: (i, 0)),
                   pl.BlockSpec((tr, HEAD_DIM), lambda i: (i, 0))],
        out_shape=[jax.ShapeDtypeStruct((m, HEAD_DIM), F32)] * 2,
        compiler_params=_cparams(("parallel",)),
        name="rope_tables",
    )(positions.reshape(m, 1), invf, sign)


class _CastJob(NamedTuple):
    src: jax.Array
    layer: int
    first_step: int
    tile: tuple
    n_col_tiles: int
    n_tiles: int


def _plan_cast_jobs(srcs, layer, n_steps):
    jobs, rest, step = [], [], 0
    for src, tile in srcs:
        _, rows, cols = src.shape
        tr, tc = min(tile[0], rows), min(tile[1], cols)
        n_tiles = (rows // tr) * (cols // tc)
        if rows % tr or cols % tc or step + n_tiles > n_steps:
            rest.append(src)
            continue
        jobs.append(_CastJob(src, layer, step, (tr, tc), cols // tc, n_tiles))
        step += n_tiles
    return jobs, rest


def _cast_job_specs(job, n_j):
    def tile_index(i, j):
        t = jnp.clip(i * n_j + j - job.first_step, 0, job.n_tiles - 1)
        return t // job.n_col_tiles, t % job.n_col_tiles
    in_spec = pl.BlockSpec((None,) + job.tile, lambda i, j: (job.layer,) + tile_index(i, j))
    out_spec = pl.BlockSpec(job.tile, tile_index)
    return in_spec, out_spec, jax.ShapeDtypeStruct(job.src.shape[1:], BF16)


def _run_cast_jobs(jobs, src_refs, dst_refs):
    step = pl.program_id(0) * pl.num_programs(1) + pl.program_id(1)
    for job, src_ref, dst_ref in zip(jobs, src_refs, dst_refs):
        @pl.when((step >= job.first_step) & (step < job.first_step + job.n_tiles))
        def _():
            dst_ref[...] = src_ref[...].astype(BF16)


def _in_proj_kernel(x_ref, w_ref, *refs, jobs):
    o_ref = refs[len(jobs)]
    o_ref[...] = jnp.dot(x_ref[...], w_ref[...], preferred_element_type=F32).astype(o_ref.dtype)
    _run_cast_jobs(jobs, refs[:len(jobs)], refs[len(jobs) + 1:])


def _in_proj(x, w, skip_start, skip_cols, cast_srcs, cast_layer, tm=1024, tn=1024):
    m, k = x.shape
    n = w.shape[1] - skip_cols
    tm, tn = min(tm, m), min(tn, n, skip_cols)
    first, skip = skip_start // tn, skip_cols // tn
    grid = (m // tm, n // tn)
    jobs, rest = _plan_cast_jobs(cast_srcs, cast_layer, grid[0] * grid[1])
    specs = [_cast_job_specs(job, grid[1]) for job in jobs]
    outs = pl.pallas_call(
        functools.partial(_in_proj_kernel, jobs=jobs),
        grid=grid,
        in_specs=[pl.BlockSpec((tm, k), lambda i, j: (i, 0)),
                  pl.BlockSpec((k, tn), lambda i, j: (0, jnp.where(j >= first, j + skip, j)))]
                 + [s[0] for s in specs],
        out_specs=[pl.BlockSpec((tm, tn), lambda i, j: (i, j))] + [s[1] for s in specs],
        out_shape=[jax.ShapeDtypeStruct((m, n), BF16)] + [s[2] for s in specs],
        compiler_params=_cparams(("arbitrary", "arbitrary")),
        name="in_proj",
    )(x, w, *[job.src for job in jobs])
    by_src = {id(job.src): out for job, out in zip(jobs, outs[1:])}
    by_src.update({id(src): src[cast_layer].astype(BF16) for src in rest})
    return outs[0], [by_src[id(src)] for src, _ in cast_srcs]


def _matmul_rope_kernel(x_ref, w_ref, cos_ref, sin_ref, o_ref, *, q_col_blocks, q_scale):
    acc = jnp.dot(x_ref[...], w_ref[...], preferred_element_type=F32)
    scale = jnp.where(pl.program_id(1) < q_col_blocks, q_scale, 1.0).astype(F32)
    cos = cos_ref[...] * scale
    sin = sin_ref[...] * scale
    for c in range(acc.shape[1] // HEAD_DIM):
        t = acc[:, c * HEAD_DIM:(c + 1) * HEAD_DIM]
        swapped = pltpu.roll(t, HEAD_DIM // 2, axis=1)
        o_ref[:, c * HEAD_DIM:(c + 1) * HEAD_DIM] = (t * cos + swapped * sin).astype(o_ref.dtype)


def _qk_proj_rope(x, w, col_start, q_cols, cos, sin, tm=1024, tn=1024):
    m, k = x.shape
    n = 2 * q_cols
    tm, tn = min(tm, m), min(tn, q_cols)
    first = col_start // tn
    return pl.pallas_call(
        functools.partial(_matmul_rope_kernel, q_col_blocks=q_cols // tn,
                          q_scale=HEAD_DIM ** -0.5 * LOG2_E),
        grid=(m // tm, n // tn),
        in_specs=[pl.BlockSpec((tm, k), lambda i, j: (i, 0)),
                  pl.BlockSpec((k, tn), lambda i, j: (0, first + j)),
                  pl.BlockSpec((tm, HEAD_DIM), lambda i, j: (i, 0)),
                  pl.BlockSpec((tm, HEAD_DIM), lambda i, j: (i, 0))],
        out_specs=pl.BlockSpec((tm, tn), lambda i, j: (i, j)),
        out_shape=jax.ShapeDtypeStruct((m, n), BF16),
        compiler_params=_cparams(("parallel", "parallel")),
        name="qk_proj_rope",
    )(x, w, cos, sin)


def _matmul_residual_kernel(a_ref, w_ref, r_ref, o_ref):
    o_ref[...] = r_ref[...] + jnp.dot(a_ref[...], w_ref[...], preferred_element_type=F32)


def _out_proj_residual(a, w, resid, tm=1024, tn=1024):
    m, k = a.shape
    n = w.shape[1]
    tm, tn = min(tm, m), min(tn, n)
    return pl.pallas_call(
        _matmul_residual_kernel,
        grid=(m // tm, n // tn),
        in_specs=[pl.BlockSpec((tm, k), lambda i, j: (i, 0)),
                  pl.BlockSpec((k, tn), lambda i, j: (0, j)),
                  pl.BlockSpec((tm, tn), lambda i, j: (i, j))],
        out_specs=pl.BlockSpec((tm, tn), lambda i, j: (i, j)),
        out_shape=jax.ShapeDtypeStruct((m, n), F32),
        compiler_params=_cparams(("parallel", "parallel")),
        name="out_proj_residual",
    )(a, w, resid)


def _rglru_kernel(ax_ref, ag_ref, cw_ref, cb_ref, wg_ref, brg_ref, big_ref, lam_ref, o_ref,
                  tail, a_s, u_s, h_s, state, *, n_blocks, block):
    ts = ax_ref.shape[0]

    @pl.when(pl.program_id(1) == 0)
    def _():
        tail[...] = jnp.zeros_like(tail)
        state[...] = jnp.zeros_like(state)

    for n in range(n_blocks):
        cols = slice(n * block, (n + 1) * block)
        x = ax_ref[:, cols].astype(F32)
        xc = cb_ref[:, cols] + _causal_taps(
            x, tail[:, cols], [cw_ref[j:j + 1, cols] for j in range(LRU_CONV)])
        tail[:, cols] = x[ts - SUBLANES:, :]
        g = jnp.dot(xc.astype(BF16), wg_ref[n], preferred_element_type=F32)
        r = _sigmoid(g[:, :block] + brg_ref[:, cols])
        i = _sigmoid(g[:, block:] + big_ref[:, cols])
        lam = lam_ref[:, cols]
        log_sig = jnp.minimum(lam, 0.0) - jnp.log1p(jnp.exp(-jnp.abs(lam)))
        a = jnp.exp2(r * (log_sig * (LRU_C * LOG2_E)))
        a_s[:, cols] = a
        u_s[:, cols] = xc * i * jnp.sqrt(_one_minus_exp(r * (log_sig * (-2.0 * LRU_C)), a * a))

    def group(gi, h):
        base = pl.multiple_of(gi * SUBLANES, SUBLANES)
        for r8 in range(SUBLANES):
            h = a_s[pl.ds(base + r8, 1), :] * h + u_s[pl.ds(base + r8, 1), :]
            h_s[pl.ds(base + r8, 1), :] = h
        return h

    state[...] = lax.fori_loop(0, ts // SUBLANES, group, state[...])
    o_ref[...] = (h_s[...] * _silu(ag_ref[...].astype(F32))).astype(o_ref.dtype)


def _rglru(p, conv_w, conv_b, w_gates, b_rg, b_ig, lru_lambda, batch, seq, width, ts=256):
    n_blocks, block = w_gates.shape[0], w_gates.shape[1]
    ts = min(ts, seq)
    nt = seq // ts
    row = lambda b, t: b * nt + t
    vec = pl.BlockSpec((1, width), lambda b, t: (0, 0))
    return pl.pallas_call(
        functools.partial(_rglru_kernel, n_blocks=n_blocks, block=block),
        grid=(batch, nt),
        in_specs=[pl.BlockSpec((ts, width), lambda b, t: (row(b, t), 0)),
                  pl.BlockSpec((ts, width), lambda b, t: (row(b, t), 1)),
                  pl.BlockSpec((LRU_CONV, width), lambda b, t: (0, 0)),
                  vec,
                  pl.BlockSpec((n_blocks, block, 2 * block), lambda b, t: (0, 0, 0)),
                  vec, vec, vec],
        out_specs=pl.BlockSpec((ts, width), lambda b, t: (row(b, t), 0)),
        out_shape=jax.ShapeDtypeStruct((batch * seq, width), BF16),
        scratch_shapes=[pltpu.VMEM((SUBLANES, width), F32),
                        pltpu.VMEM((ts, width), F32),
                        pltpu.VMEM((ts, width), F32),
                        pltpu.VMEM((ts, width), F32),
                        pltpu.VMEM((1, width), F32)],
        compiler_params=_cparams(("parallel", "arbitrary")),
        name="rglru",
    )(p, p, conv_w, conv_b.reshape(1, width), w_gates, b_rg.reshape(1, width),
      b_ig.reshape(1, width), lru_lambda.reshape(1, width))


def _diff_attn_kernel(q_ref, k_ref, v_ref, g_ref, lam_ref, sw_ref, o_ref, m_s, l_s, acc_s):
    qi = pl.program_id(2)
    tq = q_ref.shape[0]
    n_blk = tq // LANES

    m_s[...] = jnp.full_like(m_s, MASK_VALUE)
    l_s[...] = jnp.zeros_like(l_s)
    acc_s[...] = jnp.zeros_like(acc_s)

    def chunk(j, diagonal):
        k0 = pl.multiple_of(j * tq, tq)
        v = v_ref[pl.ds(k0, tq), :]
        if diagonal:
            lower = (lax.broadcasted_iota(jnp.int32, (LANES, LANES), 1)
                     <= lax.broadcasted_iota(jnp.int32, (LANES, LANES), 0))
        for c in range(2):
            hd = slice(c * HEAD_DIM, (c + 1) * HEAD_DIM)
            s = lax.dot_general(q_ref[:, hd], k_ref[pl.ds(k0, tq), hd], (((1,), (1,)), ((), ())),
                                preferred_element_type=F32)
            m_prev = m_s[c]
            p_rows, l_rows, m_rows, alpha_rows = [], [], [], []
            for rb in range(n_blk):
                rows = slice(rb * LANES, (rb + 1) * LANES)
                live = range(rb + 1) if diagonal else range(n_blk)
                blocks = [s[rows, n * LANES:(n + 1) * LANES] for n in live]
                if diagonal:
                    blocks[rb] = jnp.where(lower, blocks[rb], MASK_VALUE)
                m_cur = blocks[0]
                for blk in blocks[1:]:
                    m_cur = jnp.maximum(m_cur, blk)
                m_new = jnp.maximum(m_prev[rows], jnp.max(m_cur, axis=-1, keepdims=True))
                alpha = jnp.exp2(m_prev[rows] - m_new)
                parts = [jnp.exp2(blk - m_new) for blk in blocks]
                l_new = alpha * l_s[c, rows, :]
                for part in parts:
                    l_new = l_new + part
                parts += [jnp.zeros((LANES, LANES), F32)] * (n_blk - len(parts))
                p_rows.append(jnp.concatenate(parts, axis=-1).astype(BF16))
                l_rows.append(l_new)
                m_rows.append(m_new)
                alpha_rows.append(alpha)
            l_s[c] = jnp.concatenate(l_rows, axis=0)
            m_s[c] = jnp.concatenate(m_rows, axis=0)
            alpha = jnp.concatenate(alpha_rows, axis=0)
            alpha_wide = jnp.concatenate([alpha] * (VALUE_DIM // LANES), axis=-1)
            acc_s[c] = alpha_wide * acc_s[c] + jnp.dot(jnp.concatenate(p_rows, axis=0), v,
                                                        preferred_element_type=F32)

    def below_diagonal(u, carry):
        for t in range(GROUP):
            chunk(GROUP * u + t, diagonal=False)
        return carry

    lax.fori_loop(0, qi // GROUP, below_diagonal, 0)

    for left in range(GROUP):
        @pl.when(qi % GROUP == left)
        def _():
            for t in range(left):
                chunk(qi - left + t, diagonal=False)
            chunk(qi, diagonal=True)

    lam_init = lam_ref[4:5, 0:1]
    lam = (jnp.exp(jnp.sum(lam_ref[0:1, :] * lam_ref[1:2, :], axis=-1, keepdims=True))
           - jnp.exp(jnp.sum(lam_ref[2:3, :] * lam_ref[3:4, :], axis=-1, keepdims=True))
           + lam_init)
    l0 = jnp.sum(l_s[0], axis=-1, keepdims=True)
    l1 = jnp.sum(l_s[1], axis=-1, keepdims=True)
    o = acc_s[0] / l0 - lam * (acc_s[1] / l1)
    o = o * lax.rsqrt(jnp.mean(o * o, axis=-1, keepdims=True) + SUBLN_EPS)
    o = o * sw_ref[...] * (1.0 - lam_init)
    o_ref[...] = (o * _silu(g_ref[...].astype(F32))).astype(o_ref.dtype)


def _diff_attention(qk, p, lam_rows, subln_w, batch, seq, heads, v_col, g_col, tq=512):
    tq = min(tq, seq)
    nq = seq // tq
    return pl.pallas_call(
        _diff_attn_kernel,
        grid=(batch, heads, nq),
        in_specs=[pl.BlockSpec((tq, VALUE_DIM), lambda b, h, i: (b * nq + i, h)),
                  pl.BlockSpec((seq, VALUE_DIM), lambda b, h, i: (b, heads + h)),
                  pl.BlockSpec((seq, VALUE_DIM), lambda b, h, i: (b, v_col + h)),
                  pl.BlockSpec((tq, VALUE_DIM), lambda b, h, i: (b * nq + i, g_col + h)),
                  pl.BlockSpec((SUBLANES, HEAD_DIM), lambda b, h, i: (0, 0)),
                  pl.BlockSpec((1, VALUE_DIM), lambda b, h, i: (0, 0))],
        out_specs=pl.BlockSpec((tq, VALUE_DIM), lambda b, h, i: (b * nq + i, h)),
        out_shape=jax.ShapeDtypeStruct((batch * seq, heads * VALUE_DIM), BF16),
        scratch_shapes=[pltpu.VMEM((2, tq, LANES), F32),
                        pltpu.VMEM((2, tq, LANES), F32),
                        pltpu.VMEM((2, tq, VALUE_DIM), F32)],
        compiler_params=_cparams(("parallel", "parallel", "arbitrary")),
        name="diff_attention",
    )(qk, qk, p, p, lam_rows, subln_w.reshape(1, VALUE_DIM))


def _sconv_kernel(b_ref, c_ref, x_ref, g_ref, w_ref, o_ref, tail):
    ts = b_ref.shape[0]

    @pl.when(pl.program_id(1) == 0)
    def _():
        tail[...] = jnp.zeros_like(tail)

    z = c_ref[...].astype(F32) * x_ref[...].astype(F32)
    conv = _causal_taps(z, tail[...], [w_ref[j:j + 1, :] for j in range(SCONV_K)])
    tail[...] = z[ts - SUBLANES:, :]
    o_ref[...] = (b_ref[...].astype(F32) * conv * _silu(g_ref[...].astype(F32))).astype(o_ref.dtype)


def _short_conv(p, conv_w, batch, seq, width, first_col, ts=256):
    ts = min(ts, seq)
    nt = seq // ts
    spec = lambda c: pl.BlockSpec((ts, width), lambda b, t: (b * nt + t, first_col + c))
    return pl.pallas_call(
        _sconv_kernel,
        grid=(batch, nt),
        in_specs=[spec(0), spec(1), spec(2), spec(3),
                  pl.BlockSpec((SCONV_K, width), lambda b, t: (0, 0))],
        out_specs=pl.BlockSpec((ts, width), lambda b, t: (b * nt + t, 0)),
        out_shape=jax.ShapeDtypeStruct((batch * seq, width), BF16),
        scratch_shapes=[pltpu.VMEM((SUBLANES, width), F32)],
        compiler_params=_cparams(("parallel", "arbitrary")),
        name="short_conv",
    )(p, p, p, p, conv_w)


def _merge_kernel(ya_ref, yb_ref, yc_ref, wa_ref, wb_ref, wc_ref, ga_ref, gb_ref, gc_ref,
                  ba_ref, bb_ref, bc_ref, o_ref):
    out = None
    for y_ref, w_ref, gl_ref, b_ref in ((ya_ref, wa_ref, ga_ref, ba_ref),
                                        (yb_ref, wb_ref, gb_ref, bb_ref),
                                        (yc_ref, wc_ref, gc_ref, bc_ref)):
        gate = _sigmoid(gl_ref[...].astype(F32) + b_ref[...])
        term = gate * jnp.dot(y_ref[...], w_ref[...], preferred_element_type=F32)
        out = term if out is None else out + term
    o_ref[...] = out.astype(o_ref.dtype)


def _merge(ya, yb, yc, w_branch, p, gate_b, gl_col, tm=1024, tn=512):
    m, width = ya.shape
    d = w_branch.shape[1]
    tm, tn = min(tm, m), min(tn, d)
    y_spec = pl.BlockSpec((tm, width), lambda i, j: (i, 0))
    w_spec = lambda br: pl.BlockSpec((width, tn), lambda i, j: (br, j))
    g_spec = lambda br: pl.BlockSpec((tm, tn), lambda i, j: (i, (gl_col + br * d) // tn + j))
    b_spec = lambda br: pl.BlockSpec((None, 1, tn), lambda i, j: (br, 0, j))
    gate_b3 = gate_b.reshape(N_BRANCHES, 1, d)
    return pl.pallas_call(
        _merge_kernel,
        grid=(m // tm, d // tn),
        in_specs=[y_spec, y_spec, y_spec, w_spec(0), w_spec(1), w_spec(2),
                  g_spec(0), g_spec(1), g_spec(2), b_spec(0), b_spec(1), b_spec(2)],
        out_specs=pl.BlockSpec((tm, tn), lambda i, j: (i, j)),
        out_shape=jax.ShapeDtypeStruct((m, d), BF16),
        compiler_params=_cparams(("parallel", "parallel")),
        name="gated_merge",
    )(ya, yb, yc, w_branch, w_branch, w_branch, p, p, p, gate_b3, gate_b3, gate_b3)


def kernel(x, positions, norm_w, w_in, gate_b, conv_a_w, conv_a_b, w_rg, b_rg, w_ig, b_ig,
           lru_lambda, lam_q1, lam_k1, lam_q2, lam_k2, subln_w, conv_c_w, w_branch, w_out,
           final_norm_w):
    batch, seq, d = x.shape
    depth = norm_w.shape[0]
    bw = w_branch.shape[2]
    heads = bw // VALUE_DIM
    m = batch * seq
    assert w_in.shape[2] == N_BRANCH_SLICES * bw + N_BRANCHES * d
    assert seq % SUBLANES == 0 and bw % VALUE_DIM == 0

    w_branch_rows = w_branch.reshape(depth, N_BRANCHES * bw, d)
    w_in_l = w_in[0].astype(BF16)
    w_branch_l = w_branch_rows[0].astype(BF16)
    w_out_l = w_out[0].astype(BF16)
    w_gates_bf = jnp.concatenate([w_rg, w_ig], axis=-1).astype(BF16)

    cos, sin = _rope_tables(positions)
    xf = x.reshape(m, d)
    for l in range(depth):
        lam_init = 0.8 - 0.6 * math.exp(-0.3 * l)
        lam_rows = jnp.concatenate(
            [lam_q1[l][None], lam_k1[l][None], lam_q2[l][None], lam_k2[l][None],
             jnp.full((1, HEAD_DIM), lam_init, F32), jnp.zeros((SUBLANES - 5, HEAD_DIM), F32)], axis=0)

        h = _rmsnorm(xf, norm_w[l], BF16)
        last = l == depth - 1
        next_srcs = [] if last else [(w_in, CAST_TILE_W_IN), (w_branch_rows, CAST_TILE_SMALL),
                                     (w_out, CAST_TILE_SMALL)]
        p, next_w = _in_proj(h, w_in_l, skip_start=2 * bw, skip_cols=2 * bw,
                             cast_srcs=next_srcs, cast_layer=l + 1)
        qk = _qk_proj_rope(h, w_in_l, col_start=2 * bw, q_cols=bw, cos=cos, sin=sin)

        ya = _rglru(p, conv_a_w[l], conv_a_b[l], w_gates_bf[l], b_rg[l], b_ig[l], lru_lambda[l],
                    batch, seq, bw)
        yb = _diff_attention(qk, p, lam_rows, subln_w[l], batch, seq, heads,
                             v_col=2 * heads, g_col=3 * heads)
        yc = _short_conv(p, conv_c_w[l], batch, seq, bw, first_col=4)
        merged = _merge(ya, yb, yc, w_branch_l, p, gate_b[l], gl_col=8 * bw)
        xf = _out_proj_residual(merged, w_out_l, xf)
        if not last:
            w_in_l, w_branch_l, w_out_l = next_w
    return _rmsnorm(xf, final_norm_w, F32).reshape(batch, seq, d)
```

```python
import functools
import math
from typing import NamedTuple

import jax
import jax.numpy as jnp
from jax import lax
from jax.experimental import pallas as pl
from jax.experimental.pallas import tpu as pltpu

F32 = jnp.float32
BF16 = jnp.bfloat16

HEAD_DIM = 128
VALUE_DIM = 2 * HEAD_DIM
LRU_C = 8.0
LRU_CONV = 4
SCONV_K = 3
ROPE_THETA = 10000.0
NORM_EPS = 1e-6
SUBLN_EPS = 1e-5
N_BRANCHES = 3
N_BRANCH_SLICES = 10

V7X_VMEM_BYTES = 64 * 1024 * 1024
VMEM_LIMIT = V7X_VMEM_BYTES - 8 * 1024 * 1024
SUBLANES = 8
LANES = 128
MASK_VALUE = -1e30
LOG2_E = 1.4426950408889634
GROUP = 8
CAST_TILE_W_IN = (512, 1024)
CAST_TILE_SMALL = (256, 1024)


def _cparams(semantics):
    return pltpu.CompilerParams(dimension_semantics=semantics, vmem_limit_bytes=VMEM_LIMIT)


def _sigmoid(x):
    return 1.0 / (1.0 + jnp.exp(-x))


def _silu(x):
    return x * _sigmoid(x)


def _one_minus_exp(z, exp_neg_z):
    poly = 1.0 - (z * 0.5) * (1.0 - z * (1.0 / 3.0))
    return jnp.where(z < 1.0 / 64.0, z * poly, 1.0 - exp_neg_z)


def _causal_taps(x, prev_tail, conv_w_rows):
    k_width = len(conv_w_rows)
    out = conv_w_rows[k_width - 1] * x
    head_rows = lax.broadcasted_iota(jnp.int32, (SUBLANES, x.shape[1]), 0)
    for shift in range(1, k_width):
        rolled = pltpu.roll(x, shift, axis=0)
        head = jnp.where(head_rows < shift, pltpu.roll(prev_tail, shift, axis=0), rolled[0:SUBLANES])
        shifted = jnp.concatenate([head, rolled[SUBLANES:]], axis=0)
        out = out + conv_w_rows[k_width - 1 - shift] * shifted
    return out


def _rmsnorm_kernel(x_ref, w_ref, o_ref, *, eps):
    x = x_ref[...].astype(F32)
    ms = jnp.mean(x * x, axis=-1, keepdims=True)
    o_ref[...] = (x * lax.rsqrt(ms + eps) * w_ref[...]).astype(o_ref.dtype)


def _rmsnorm(x, w, out_dtype, eps=NORM_EPS):
    m, d = x.shape
    tr = min(512, m)
    return pl.pallas_call(
        functools.partial(_rmsnorm_kernel, eps=eps),
        grid=(m // tr,),
        in_specs=[pl.BlockSpec((tr, d), lambda i: (i, 0)),
                  pl.BlockSpec((1, d), lambda i: (0, 0))],
        out_specs=pl.BlockSpec((tr, d), lambda i: (i, 0)),
        out_shape=jax.ShapeDtypeStruct((m, d), out_dtype),
        compiler_params=_cparams(("parallel",)),
        name="rmsnorm",
    )(x, w.reshape(1, d))


def _rope_table_kernel(pos_ref, invf_ref, sign_ref, cos_ref, sin_ref):
    ang = pos_ref[...].astype(F32) * invf_ref[...]
    cos_ref[...] = jnp.cos(ang)
    sin_ref[...] = jnp.sin(ang) * sign_ref[...]


def _rope_tables(positions):
    m = positions.size
    half = HEAD_DIM // 2
    inv_freq = ROPE_THETA ** (-jnp.arange(0, half, dtype=F32) * (2.0 / HEAD_DIM))
    invf = jnp.concatenate([inv_freq, inv_freq]).reshape(1, HEAD_DIM)
    sign = jnp.concatenate([-jnp.ones((half,), F32), jnp.ones((half,), F32)]).reshape(1, HEAD_DIM)
    tr = min(512, m)
    return pl.pallas_call(
        _rope_table_kernel,
        grid=(m // tr,),
        in_specs=[pl.BlockSpec((tr, 1), lambda i: (i, 0)),
                  pl.BlockSpec((1, HEAD_DIM), lambda i: (0, 0)),
                  pl.BlockSpec((1, HEAD_DIM), lambda i: (0, 0))],
        out_specs=[pl.BlockSpec((tr, HEAD_DIM), lambda i: (i, 0)),
                   pl.BlockSpec((tr, HEAD_DIM), lambda i: (i, 0))],
        out_shape=[jax.ShapeDtypeStruct((m, HEAD_DIM), F32)] * 2,
        compiler_params=_cparams(("parallel",)),
        name="rope_tables",
    )(positions.reshape(m, 1), invf, sign)


class _CastJob(NamedTuple):
    src: jax.Array
    layer: int
    first_step: int
    tile: tuple
    n_col_tiles: int
    n_tiles: int


def _plan_cast_jobs(srcs, layer, n_steps):
    jobs, rest, step = [], [], 0
    for src, tile in srcs:
        _, rows, cols = src.shape
        tr, tc = min(tile[0], rows), min(tile[1], cols)
        n_tiles = (rows // tr) * (cols // tc)
        if rows % tr or cols % tc or step + n_tiles > n_steps:
            rest.append(src)
            continue
        jobs.append(_CastJob(src, layer, step, (tr, tc), cols // tc, n_tiles))
        step += n_tiles
    return jobs, rest


def _cast_job_specs(job, n_j):
    def tile_index(i, j):
        t = jnp.clip(i * n_j + j - job.first_step, 0, job.n_tiles - 1)
        return t // job.n_col_tiles, t % job.n_col_tiles
    in_spec = pl.BlockSpec((None,) + job.tile, lambda i, j: (job.layer,) + tile_index(i, j))
    out_spec = pl.BlockSpec(job.tile, tile_index)
    return in_spec, out_spec, jax.ShapeDtypeStruct(job.src.shape[1:], BF16)


def _run_cast_jobs(jobs, src_refs, dst_refs):
    step = pl.program_id(0) * pl.num_programs(1) + pl.program_id(1)
    for job, src_ref, dst_ref in zip(jobs, src_refs, dst_refs):
        @pl.when((step >= job.first_step) & (step < job.first_step + job.n_tiles))
        def _():
            dst_ref[...] = src_ref[...].astype(BF16)


def _in_proj_kernel(x_ref, w_ref, *refs, jobs):
    o_ref = refs[len(jobs)]
    o_ref[...] = jnp.dot(x_ref[...], w_ref[...], preferred_element_type=F32).astype(o_ref.dtype)
    _run_cast_jobs(jobs, refs[:len(jobs)], refs[len(jobs) + 1:])


def _in_proj(x, w, skip_start, skip_cols, cast_srcs, cast_layer, tm=1024, tn=1024):
    m, k = x.shape
    n = w.shape[1] - skip_cols
    tm, tn = min(tm, m), min(tn, n, skip_cols)
    first, skip = skip_start // tn, skip_cols // tn
    grid = (m // tm, n // tn)
    jobs, rest = _plan_cast_jobs(cast_srcs, cast_layer, grid[0] * grid[1])
    specs = [_cast_job_specs(job, grid[1]) for job in jobs]
    outs = pl.pallas_call(
        functools.partial(_in_proj_kernel, jobs=jobs),
        grid=grid,
        in_specs=[pl.BlockSpec((tm, k), lambda i, j: (i, 0)),
                  pl.BlockSpec((k, tn), lambda i, j: (0, jnp.where(j >= first, j + skip, j)))]
                 + [s[0] for s in specs],
        out_specs=[pl.BlockSpec((tm, tn), lambda i, j: (i, j))] + [s[1] for s in specs],
        out_shape=[jax.ShapeDtypeStruct((m, n), BF16)] + [s[2] for s in specs],
        compiler_params=_cparams(("arbitrary", "arbitrary")),
        name="in_proj",
    )(x, w, *[job.src for job in jobs])
    by_src = {id(job.src): out for job, out in zip(jobs, outs[1:])}
    by_src.update({id(src): src[cast_layer].astype(BF16) for src in rest})
    return outs[0], [by_src[id(src)] for src, _ in cast_srcs]


def _matmul_rope_kernel(x_ref, w_ref, cos_ref, sin_ref, o_ref, *, q_col_blocks, q_scale):
    acc = jnp.dot(x_ref[...], w_ref[...], preferred_element_type=F32)
    scale = jnp.where(pl.program_id(1) < q_col_blocks, q_scale, 1.0).astype(F32)
    cos = cos_ref[...] * scale
    sin = sin_ref[...] * scale
    for c in range(acc.shape[1] // HEAD_DIM):
        t = acc[:, c * HEAD_DIM:(c + 1) * HEAD_DIM]
        swapped = pltpu.roll(t, HEAD_DIM // 2, axis=1)
        o_ref[:, c * HEAD_DIM:(c + 1) * HEAD_DIM] = (t * cos + swapped * sin).astype(o_ref.dtype)


def _qk_proj_rope(x, w, col_start, q_cols, cos, sin, tm=1024, tn=1024):
    m, k = x.shape
    n = 2 * q_cols
    tm, tn = min(tm, m), min(tn, q_cols)
    first = col_start // tn
    return pl.pallas_call(
        functools.partial(_matmul_rope_kernel, q_col_blocks=q_cols // tn,
                          q_scale=HEAD_DIM ** -0.5 * LOG2_E),
        grid=(m // tm, n // tn),
        in_specs=[pl.BlockSpec((tm, k), lambda i, j: (i, 0)),
                  pl.BlockSpec((k, tn), lambda i, j: (0, first + j)),
                  pl.BlockSpec((tm, HEAD_DIM), lambda i, j: (i, 0)),
                  pl.BlockSpec((tm, HEAD_DIM), lambda i, j: (i, 0))],
        out_specs=pl.BlockSpec((tm, tn), lambda i, j: (i, j)),
        out_shape=jax.ShapeDtypeStruct((m, n), BF16),
        compiler_params=_cparams(("parallel", "parallel")),
        name="qk_proj_rope",
    )(x, w, cos, sin)


def _matmul_residual_kernel(a_ref, w_ref, r_ref, o_ref):
    o_ref[...] = r_ref[...] + jnp.dot(a_ref[...], w_ref[...], preferred_element_type=F32)


def _out_proj_residual(a, w, resid, tm=1024, tn=1024):
    m, k = a.shape
    n = w.shape[1]
    tm, tn = min(tm, m), min(tn, n)
    return pl.pallas_call(
        _matmul_residual_kernel,
        grid=(m // tm, n // tn),
        in_specs=[pl.BlockSpec((tm, k), lambda i, j: (i, 0)),
                  pl.BlockSpec((k, tn), lambda i, j: (0, j)),
                  pl.BlockSpec((tm, tn), lambda i, j: (i, j))],
        out_specs=pl.BlockSpec((tm, tn), lambda i, j: (i, j)),
        out_shape=jax.ShapeDtypeStruct((m, n), F32),
        compiler_params=_cparams(("parallel", "parallel")),
        name="out_proj_residual",
    )(a, w, resid)


def _rglru_kernel(ax_ref, ag_ref, cw_ref, cb_ref, wg_ref, brg_ref, big_ref, lam_ref, o_ref,
                  tail, a_s, u_s, h_s, state, *, n_blocks, block):
    ts = ax_ref.shape[0]

    @pl.when(pl.program_id(1) == 0)
    def _():
        tail[...] = jnp.zeros_like(tail)
        state[...] = jnp.zeros_like(state)

    for n in range(n_blocks):
        cols = slice(n * block, (n + 1) * block)
        x = ax_ref[:, cols].astype(F32)
        xc = cb_ref[:, cols] + _causal_taps(
            x, tail[:, cols], [cw_ref[j:j + 1, cols] for j in range(LRU_CONV)])
        tail[:, cols] = x[ts - SUBLANES:, :]
        g = jnp.dot(xc.astype(BF16), wg_ref[n], preferred_element_type=F32)
        r = _sigmoid(g[:, :block] + brg_ref[:, cols])
        i = _sigmoid(g[:, block:] + big_ref[:, cols])
        lam = lam_ref[:, cols]
        log_sig = jnp.minimum(lam, 0.0) - jnp.log1p(jnp.exp(-jnp.abs(lam)))
        a = jnp.exp2(r * (log_sig * (LRU_C * LOG2_E)))
        a_s[:, cols] = a
        u_s[:, cols] = xc * i * jnp.sqrt(_one_minus_exp(r * (log_sig * (-2.0 * LRU_C)), a * a))

    def group(gi, h):
        base = pl.multiple_of(gi * SUBLANES, SUBLANES)
        for r8 in range(SUBLANES):
            h = a_s[pl.ds(base + r8, 1), :] * h + u_s[pl.ds(base + r8, 1), :]
            h_s[pl.ds(base + r8, 1), :] = h
        return h

    state[...] = lax.fori_loop(0, ts // SUBLANES, group, state[...])
    o_ref[...] = (h_s[...] * _silu(ag_ref[...].astype(F32))).astype(o_ref.dtype)


def _rglru(p, conv_w, conv_b, w_gates, b_rg, b_ig, lru_lambda, batch, seq, width, ts=256):
    n_blocks, block = w_gates.shape[0], w_gates.shape[1]
    ts = min(ts, seq)
    nt = seq // ts
    row = lambda b, t: b * nt + t
    vec = pl.BlockSpec((1, width), lambda b, t: (0, 0))
    return pl.pallas_call(
        functools.partial(_rglru_kernel, n_blocks=n_blocks, block=block),
        grid=(batch, nt),
        in_specs=[pl.BlockSpec((ts, width), lambda b, t: (row(b, t), 0)),
                  pl.BlockSpec((ts, width), lambda b, t: (row(b, t), 1)),
                  pl.BlockSpec((LRU_CONV, width), lambda b, t: (0, 0)),
                  vec,
                  pl.BlockSpec((n_blocks, block, 2 * block), lambda b, t: (0, 0, 0)),
                  vec, vec, vec],
        out_specs=pl.BlockSpec((ts, width), lambda b, t: (row(b, t), 0)),
        out_shape=jax.ShapeDtypeStruct((batch * seq, width), BF16),
        scratch_shapes=[pltpu.VMEM((SUBLANES, width), F32),
                        pltpu.VMEM((ts, width), F32),
                        pltpu.VMEM((ts, width), F32),
                        pltpu.VMEM((ts, width), F32),
                        pltpu.VMEM((1, width), F32)],
        compiler_params=_cparams(("parallel", "arbitrary")),
        name="rglru",
    )(p, p, conv_w, conv_b.reshape(1, width), w_gates, b_rg.reshape(1, width),
      b_ig.reshape(1, width), lru_lambda.reshape(1, width))


def _diff_attn_kernel(q_ref, k_ref, v_ref, g_ref, lam_ref, sw_ref, o_ref, m_s, l_s, acc_s, *,
                      group, n_q_blocks):
    qi = pl.program_id(2)
    tq = q_ref.shape[0]
    n_blk = tq // LANES

    m_s[...] = jnp.full_like(m_s, MASK_VALUE)
    l_s[...] = jnp.zeros_like(l_s)
    acc_s[...] = jnp.zeros_like(acc_s)

    def chunk(j, diagonal):
        k0 = pl.multiple_of(j * tq, tq)
        v = v_ref[pl.ds(k0, tq), :]
        if diagonal:
            lower = (lax.broadcasted_iota(jnp.int32, (LANES, LANES), 1)
                     <= lax.broadcasted_iota(jnp.int32, (LANES, LANES), 0))
        for c in range(2):
            hd = slice(c * HEAD_DIM, (c + 1) * HEAD_DIM)
            s = lax.dot_general(q_ref[:, hd], k_ref[pl.ds(k0, tq), hd], (((1,), (1,)), ((), ())),
                                preferred_element_type=F32)
            m_prev = m_s[c]
            p_rows, l_rows, m_rows, alpha_rows = [], [], [], []
            for rb in range(n_blk):
                rows = slice(rb * LANES, (rb + 1) * LANES)
                live = range(rb + 1) if diagonal else range(n_blk)
                blocks = [s[rows, n * LANES:(n + 1) * LANES] for n in live]
                if diagonal:
                    blocks[rb] = jnp.where(lower, blocks[rb], MASK_VALUE)
                m_cur = blocks[0]
                for blk in blocks[1:]:
                    m_cur = jnp.maximum(m_cur, blk)
                m_new = jnp.maximum(m_prev[rows], jnp.max(m_cur, axis=-1, keepdims=True))
                alpha = jnp.exp2(m_prev[rows] - m_new)
                parts = [jnp.exp2(blk - m_new) for blk in blocks]
                l_new = alpha * l_s[c, rows, :]
                for part in parts:
                    l_new = l_new + part
                parts += [jnp.zeros((LANES, LANES), F32)] * (n_blk - len(parts))
                p_rows.append(jnp.concatenate(parts, axis=-1).astype(BF16))
                l_rows.append(l_new)
                m_rows.append(m_new)
                alpha_rows.append(alpha)
            l_s[c] = jnp.concatenate(l_rows, axis=0)
            m_s[c] = jnp.concatenate(m_rows, axis=0)
            alpha = jnp.concatenate(alpha_rows, axis=0)
            alpha_wide = jnp.concatenate([alpha] * (VALUE_DIM // LANES), axis=-1)
            acc_s[c] = alpha_wide * acc_s[c] + jnp.dot(jnp.concatenate(p_rows, axis=0), v,
                                                        preferred_element_type=F32)

    def below_diagonal(u, carry):
        for t in range(group):
            chunk(group * u + t, diagonal=False)
        return carry

    if group < n_q_blocks:
        lax.fori_loop(0, qi // group, below_diagonal, 0)

    for left in range(group):
        @pl.when(qi % group == left)
        def _():
            for t in range(left):
                chunk(qi - left + t, diagonal=False)
            chunk(qi, diagonal=True)

    lam_init = lam_ref[4:5, 0:1]
    lam = (jnp.exp(jnp.sum(lam_ref[0:1, :] * lam_ref[1:2, :], axis=-1, keepdims=True))
           - jnp.exp(jnp.sum(lam_ref[2:3, :] * lam_ref[3:4, :], axis=-1, keepdims=True))
           + lam_init)
    l0 = jnp.sum(l_s[0], axis=-1, keepdims=True)
    l1 = jnp.sum(l_s[1], axis=-1, keepdims=True)
    o = acc_s[0] / l0 - lam * (acc_s[1] / l1)
    o = o * lax.rsqrt(jnp.mean(o * o, axis=-1, keepdims=True) + SUBLN_EPS)
    o = o * sw_ref[...] * (1.0 - lam_init)
    o_ref[...] = (o * _silu(g_ref[...].astype(F32))).astype(o_ref.dtype)


def _diff_attention(qk, p, lam_rows, subln_w, batch, seq, heads, v_col, g_col, tq=512):
    tq = min(tq, seq)
    nq = seq // tq
    return pl.pallas_call(
        functools.partial(_diff_attn_kernel, group=min(GROUP, nq), n_q_blocks=nq),
        grid=(batch, heads, nq),
        in_specs=[pl.BlockSpec((tq, VALUE_DIM), lambda b, h, i: (b * nq + i, h)),
                  pl.BlockSpec((seq, VALUE_DIM), lambda b, h, i: (b, heads + h)),
                  pl.BlockSpec((seq, VALUE_DIM), lambda b, h, i: (b, v_col + h)),
                  pl.BlockSpec((tq, VALUE_DIM), lambda b, h, i: (b * nq + i, g_col + h)),
                  pl.BlockSpec((SUBLANES, HEAD_DIM), lambda b, h, i: (0, 0)),
                  pl.BlockSpec((1, VALUE_DIM), lambda b, h, i: (0, 0))],
        out_specs=pl.BlockSpec((tq, VALUE_DIM), lambda b, h, i: (b * nq + i, h)),
        out_shape=jax.ShapeDtypeStruct((batch * seq, heads * VALUE_DIM), BF16),
        scratch_shapes=[pltpu.VMEM((2, tq, LANES), F32),
                        pltpu.VMEM((2, tq, LANES), F32),
                        pltpu.VMEM((2, tq, VALUE_DIM), F32)],
        compiler_params=_cparams(("parallel", "parallel", "arbitrary")),
        name="diff_attention",
    )(qk, qk, p, p, lam_rows, subln_w.reshape(1, VALUE_DIM))


def _sconv_kernel(b_ref, c_ref, x_ref, g_ref, w_ref, o_ref, tail):
    ts = b_ref.shape[0]

    @pl.when(pl.program_id(1) == 0)
    def _():
        tail[...] = jnp.zeros_like(tail)

    z = c_ref[...].astype(F32) * x_ref[...].astype(F32)
    conv = _causal_taps(z, tail[...], [w_ref[j:j + 1, :] for j in range(SCONV_K)])
    tail[...] = z[ts - SUBLANES:, :]
    o_ref[...] = (b_ref[...].astype(F32) * conv * _silu(g_ref[...].astype(F32))).astype(o_ref.dtype)


def _short_conv(p, conv_w, batch, seq, width, first_col, ts=256):
    ts = min(ts, seq)
    nt = seq // ts
    spec = lambda c: pl.BlockSpec((ts, width), lambda b, t: (b * nt + t, first_col + c))
    return pl.pallas_call(
        _sconv_kernel,
        grid=(batch, nt),
        in_specs=[spec(0), spec(1), spec(2), spec(3),
                  pl.BlockSpec((SCONV_K, width), lambda b, t: (0, 0))],
        out_specs=pl.BlockSpec((ts, width), lambda b, t: (b * nt + t, 0)),
        out_shape=jax.ShapeDtypeStruct((batch * seq, width), BF16),
        scratch_shapes=[pltpu.VMEM((SUBLANES, width), F32)],
        compiler_params=_cparams(("parallel", "arbitrary")),
        name="short_conv",
    )(p, p, p, p, conv_w)


def _merge_kernel(ya_ref, yb_ref, yc_ref, wa_ref, wb_ref, wc_ref, ga_ref, gb_ref, gc_ref,
                  ba_ref, bb_ref, bc_ref, o_ref):
    out = None
    for y_ref, w_ref, gl_ref, b_ref in ((ya_ref, wa_ref, ga_ref, ba_ref),
                                        (yb_ref, wb_ref, gb_ref, bb_ref),
                                        (yc_ref, wc_ref, gc_ref, bc_ref)):
        gate = _sigmoid(gl_ref[...].astype(F32) + b_ref[...])
        term = gate * jnp.dot(y_ref[...], w_ref[...], preferred_element_type=F32)
        out = term if out is None else out + term
    o_ref[...] = out.astype(o_ref.dtype)


def _merge(ya, yb, yc, w_branch, p, gate_b, gl_col, tm=1024, tn=512):
    m, width = ya.shape
    d = w_branch.shape[1]
    tm, tn = min(tm, m), min(tn, d)
    y_spec = pl.BlockSpec((tm, width), lambda i, j: (i, 0))
    w_spec = lambda br: pl.BlockSpec((width, tn), lambda i, j: (br, j))
    g_spec = lambda br: pl.BlockSpec((tm, tn), lambda i, j: (i, (gl_col + br * d) // tn + j))
    b_spec = lambda br: pl.BlockSpec((None, 1, tn), lambda i, j: (br, 0, j))
    gate_b3 = gate_b.reshape(N_BRANCHES, 1, d)
    return pl.pallas_call(
        _merge_kernel,
        grid=(m // tm, d // tn),
        in_specs=[y_spec, y_spec, y_spec, w_spec(0), w_spec(1), w_spec(2),
                  g_spec(0), g_spec(1), g_spec(2), b_spec(0), b_spec(1), b_spec(2)],
        out_specs=pl.BlockSpec((tm, tn), lambda i, j: (i, j)),
        out_shape=jax.ShapeDtypeStruct((m, d), BF16),
        compiler_params=_cparams(("parallel", "parallel")),
        name="gated_merge",
    )(ya, yb, yc, w_branch, w_branch, w_branch, p, p, p, gate_b3, gate_b3, gate_b3)


def kernel(x, positions, norm_w, w_in, gate_b, conv_a_w, conv_a_b, w_rg, b_rg, w_ig, b_ig,
           lru_lambda, lam_q1, lam_k1, lam_q2, lam_k2, subln_w, conv_c_w, w_branch, w_out,
           final_norm_w):
    batch, seq, d = x.shape
    depth = norm_w.shape[0]
    bw = w_branch.shape[2]
    heads = bw // VALUE_DIM
    m = batch * seq
    assert w_in.shape[2] == N_BRANCH_SLICES * bw + N_BRANCHES * d
    assert seq % SUBLANES == 0 and bw % VALUE_DIM == 0

    w_branch_rows = w_branch.reshape(depth, N_BRANCHES * bw, d)
    w_in_l = w_in[0].astype(BF16)
    w_branch_l = w_branch_rows[0].astype(BF16)
    w_out_l = w_out[0].astype(BF16)
    w_gates_bf = jnp.concatenate([w_rg, w_ig], axis=-1).astype(BF16)

    cos, sin = _rope_tables(positions)
    xf = x.reshape(m, d)
    for l in range(depth):
        lam_init = 0.8 - 0.6 * math.exp(-0.3 * l)
        lam_rows = jnp.concatenate(
            [lam_q1[l][None], lam_k1[l][None], lam_q2[l][None], lam_k2[l][None],
             jnp.full((1, HEAD_DIM), lam_init, F32), jnp.zeros((SUBLANES - 5, HEAD_DIM), F32)], axis=0)

        h = _rmsnorm(xf, norm_w[l], BF16)
        last = l == depth - 1
        next_srcs = [] if last else [(w_in, CAST_TILE_W_IN), (w_branch_rows, CAST_TILE_SMALL),
                                     (w_out, CAST_TILE_SMALL)]
        p, next_w = _in_proj(h, w_in_l, skip_start=2 * bw, skip_cols=2 * bw,
                             cast_srcs=next_srcs, cast_layer=l + 1)
        qk = _qk_proj_rope(h, w_in_l, col_start=2 * bw, q_cols=bw, cos=cos, sin=sin)

        ya = _rglru(p, conv_a_w[l], conv_a_b[l], w_gates_bf[l], b_rg[l], b_ig[l], lru_lambda[l],
                    batch, seq, bw)
        yb = _diff_attention(qk, p, lam_rows, subln_w[l], batch, seq, heads,
                             v_col=2 * heads, g_col=3 * heads)
        yc = _short_conv(p, conv_c_w[l], batch, seq, bw, first_col=4)
        merged = _merge(ya, yb, yc, w_branch_l, p, gate_b[l], gl_col=8 * bw)
        xf = _out_proj_residual(merged, w_out_l, xf)
        if not last:
            w_in_l, w_branch_l, w_out_l = next_w
    return _rmsnorm(xf, final_norm_w, F32).reshape(batch, seq, d)
```

```python
import functools
import math
from typing import NamedTuple

import jax
import jax.numpy as jnp
from jax import lax
from jax.experimental import pallas as pl
from jax.experimental.pallas import tpu as pltpu

F32 = jnp.float32
BF16 = jnp.bfloat16

HEAD_DIM = 128
VALUE_DIM = 2 * HEAD_DIM
LRU_C = 8.0
LRU_CONV = 4
SCONV_K = 3
ROPE_THETA = 10000.0
NORM_EPS = 1e-6
SUBLN_EPS = 1e-5
N_BRANCHES = 3
N_BRANCH_SLICES = 10

V7X_VMEM_BYTES = 64 * 1024 * 1024
VMEM_LIMIT = V7X_VMEM_BYTES - 8 * 1024 * 1024
SUBLANES = 8
LANES = 128
MASK_VALUE = -1e30
LOG2_E = 1.4426950408889634
GROUP = 8
CAST_TILE_W_IN = (512, 1024)
CAST_TILE_SMALL = (256, 1024)


def _cparams(semantics):
    return pltpu.CompilerParams(dimension_semantics=semantics, vmem_limit_bytes=VMEM_LIMIT)


def _sigmoid(x):
    return 1.0 / (1.0 + jnp.exp(-x))


def _silu(x):
    return x * _sigmoid(x)


def _one_minus_exp(z, exp_neg_z):
    poly = 1.0 - (z * 0.5) * (1.0 - z * (1.0 / 3.0))
    return jnp.where(z < 1.0 / 64.0, z * poly, 1.0 - exp_neg_z)


def _causal_taps(x, prev_tail, conv_w_rows):
    k_width = len(conv_w_rows)
    out = conv_w_rows[k_width - 1] * x
    head_rows = lax.broadcasted_iota(jnp.int32, (SUBLANES, x.shape[1]), 0)
    for shift in range(1, k_width):
        rolled = pltpu.roll(x, shift, axis=0)
        head = jnp.where(head_rows < shift, pltpu.roll(prev_tail, shift, axis=0), rolled[0:SUBLANES])
        shifted = jnp.concatenate([head, rolled[SUBLANES:]], axis=0)
        out = out + conv_w_rows[k_width - 1 - shift] * shifted
    return out


def _rmsnorm_kernel(x_ref, w_ref, o_ref, *, eps):
    x = x_ref[...].astype(F32)
    ms = jnp.mean(x * x, axis=-1, keepdims=True)
    o_ref[...] = (x * lax.rsqrt(ms + eps) * w_ref[...]).astype(o_ref.dtype)


def _rmsnorm(x, w, out_dtype, eps=NORM_EPS):
    m, d = x.shape
    tr = min(512, m)
    return pl.pallas_call(
        functools.partial(_rmsnorm_kernel, eps=eps),
        grid=(m // tr,),
        in_specs=[pl.BlockSpec((tr, d), lambda i: (i, 0)),
                  pl.BlockSpec((1, d), lambda i: (0, 0))],
        out_specs=pl.BlockSpec((tr, d), lambda i: (i, 0)),
        out_shape=jax.ShapeDtypeStruct((m, d), out_dtype),
        compiler_params=_cparams(("parallel",)),
        name="rmsnorm",
    )(x, w.reshape(1, d))


def _rope_table_kernel(pos_ref, invf_ref, sign_ref, cos_ref, sin_ref):
    ang = pos_ref[...].astype(F32) * invf_ref[...]
    cos_ref[...] = jnp.cos(ang)
    sin_ref[...] = jnp.sin(ang) * sign_ref[...]


def _rope_tables(positions):
    m = positions.size
    half = HEAD_DIM // 2
    inv_freq = ROPE_THETA ** (-jnp.arange(0, half, dtype=F32) * (2.0 / HEAD_DIM))
    invf = jnp.concatenate([inv_freq, inv_freq]).reshape(1, HEAD_DIM)
    sign = jnp.concatenate([-jnp.ones((half,), F32), jnp.ones((half,), F32)]).reshape(1, HEAD_DIM)
    tr = min(512, m)
    return pl.pallas_call(
        _rope_table_kernel,
        grid=(m // tr,),
        in_specs=[pl.BlockSpec((tr, 1), lambda i: (i, 0)),
                  pl.BlockSpec((1, HEAD_DIM), lambda i: (0, 0)),
                  pl.BlockSpec((1, HEAD_DIM), lambda i: (0, 0))],
        out_specs=[pl.BlockSpec((tr, HEAD_DIM), lambda i: (i, 0)),
                   pl.BlockSpec((tr, HEAD_DIM), lambda i: (i, 0))],
        out_shape=[jax.ShapeDtypeStruct((m, HEAD_DIM), F32)] * 2,
        compiler_params=_cparams(("parallel",)),
        name="rope_tables",
    )(positions.reshape(m, 1), invf, sign)


class _CastJob(NamedTuple):
    src: jax.Array
    layer: int
    first_step: int
    tile: tuple
    n_col_tiles: int
    n_tiles: int


def _plan_cast_jobs(srcs, layer, n_steps):
    jobs, rest, step = [], [], 0
    for src, tile in srcs:
        _, rows, cols = src.shape
        tr, tc = min(tile[0], rows), min(tile[1], cols)
        n_tiles = (rows // tr) * (cols // tc)
        if rows % tr or cols % tc or step + n_tiles > n_steps:
            rest.append(src)
            continue
        jobs.append(_CastJob(src, layer, step, (tr, tc), cols // tc, n_tiles))
        step += n_tiles
    return jobs, rest


def _cast_job_specs(job, n_j):
    def tile_index(i, j):
        t = jnp.clip(i * n_j + j - job.first_step, 0, job.n_tiles - 1)
        return t // job.n_col_tiles, t % job.n_col_tiles
    in_spec = pl.BlockSpec((None,) + job.tile, lambda i, j: (job.layer,) + tile_index(i, j))
    out_spec = pl.BlockSpec(job.tile, tile_index)
    return in_spec, out_spec, jax.ShapeDtypeStruct(job.src.shape[1:], BF16)


def _run_cast_jobs(jobs, src_refs, dst_refs):
    step = pl.program_id(0) * pl.num_programs(1) + pl.program_id(1)
    for job, src_ref, dst_ref in zip(jobs, src_refs, dst_refs):
        @pl.when((step >= job.first_step) & (step < job.first_step + job.n_tiles))
        def _():
            dst_ref[...] = src_ref[...].astype(BF16)


def _in_proj_kernel(x_ref, w_ref, *refs, jobs):
    o_ref = refs[len(jobs)]
    o_ref[...] = jnp.dot(x_ref[...], w_ref[...], preferred_element_type=F32).astype(o_ref.dtype)
    _run_cast_jobs(jobs, refs[:len(jobs)], refs[len(jobs) + 1:])


def _in_proj(x, w, skip_start, skip_cols, cast_srcs, cast_layer, tm=1024, tn=1024):
    m, k = x.shape
    n = w.shape[1] - skip_cols
    tm, tn = min(tm, m), min(tn, n, skip_cols)
    first, skip = skip_start // tn, skip_cols // tn
    grid = (m // tm, n // tn)
    jobs, rest = _plan_cast_jobs(cast_srcs, cast_layer, grid[0] * grid[1])
    specs = [_cast_job_specs(job, grid[1]) for job in jobs]
    outs = pl.pallas_call(
        functools.partial(_in_proj_kernel, jobs=jobs),
        grid=grid,
        in_specs=[pl.BlockSpec((tm, k), lambda i, j: (i, 0)),
                  pl.BlockSpec((k, tn), lambda i, j: (0, jnp.where(j >= first, j + skip, j)))]
                 + [s[0] for s in specs],
        out_specs=[pl.BlockSpec((tm, tn), lambda i, j: (i, j))] + [s[1] for s in specs],
        out_shape=[jax.ShapeDtypeStruct((m, n), BF16)] + [s[2] for s in specs],
        compiler_params=_cparams(("arbitrary", "arbitrary")),
        name="in_proj",
    )(x, w, *[job.src for job in jobs])
    by_src = {id(job.src): out for job, out in zip(jobs, outs[1:])}
    by_src.update({id(src): src[cast_layer].astype(BF16) for src in rest})
    return outs[0], [by_src[id(src)] for src, _ in cast_srcs]


def _matmul_rope_kernel(x_ref, w_ref, cos_ref, sin_ref, o_ref, *, q_col_blocks, q_scale):
    acc = jnp.dot(x_ref[...], w_ref[...], preferred_element_type=F32)
    scale = jnp.where(pl.program_id(1) < q_col_blocks, q_scale, 1.0).astype(F32)
    cos = cos_ref[...] * scale
    sin = sin_ref[...] * scale
    for c in range(acc.shape[1] // HEAD_DIM):
        t = acc[:, c * HEAD_DIM:(c + 1) * HEAD_DIM]
        swapped = pltpu.roll(t, HEAD_DIM // 2, axis=1)
        o_ref[:, c * HEAD_DIM:(c + 1) * HEAD_DIM] = (t * cos + swapped * sin).astype(o_ref.dtype)


def _qk_proj_rope(x, w, col_start, q_cols, cos, sin, tm=1024, tn=1024):
    m, k = x.shape
    n = 2 * q_cols
    tm, tn = min(tm, m), min(tn, q_cols)
    first = col_start // tn
    return pl.pallas_call(
        functools.partial(_matmul_rope_kernel, q_col_blocks=q_cols // tn,
                          q_scale=HEAD_DIM ** -0.5 * LOG2_E),
        grid=(m // tm, n // tn),
        in_specs=[pl.BlockSpec((tm, k), lambda i, j: (i, 0)),
                  pl.BlockSpec((k, tn), lambda i, j: (0, first + j)),
                  pl.BlockSpec((tm, HEAD_DIM), lambda i, j: (i, 0)),
                  pl.BlockSpec((tm, HEAD_DIM), lambda i, j: (i, 0))],
        out_specs=pl.BlockSpec((tm, tn), lambda i, j: (i, j)),
        out_shape=jax.ShapeDtypeStruct((m, n), BF16),
        compiler_params=_cparams(("parallel", "parallel")),
        name="qk_proj_rope",
    )(x, w, cos, sin)


def _matmul_residual_kernel(a_ref, w_ref, r_ref, o_ref):
    o_ref[...] = r_ref[...] + jnp.dot(a_ref[...], w_ref[...], preferred_element_type=F32)


def _out_proj_residual(a, w, resid, tm=1024, tn=1024):
    m, k = a.shape
    n = w.shape[1]
    tm, tn = min(tm, m), min(tn, n)
    return pl.pallas_call(
        _matmul_residual_kernel,
        grid=(m // tm, n // tn),
        in_specs=[pl.BlockSpec((tm, k), lambda i, j: (i, 0)),
                  pl.BlockSpec((k, tn), lambda i, j: (0, j)),
                  pl.BlockSpec((tm, tn), lambda i, j: (i, j))],
        out_specs=pl.BlockSpec((tm, tn), lambda i, j: (i, j)),
        out_shape=jax.ShapeDtypeStruct((m, n), F32),
        compiler_params=_cparams(("parallel", "parallel")),
        name="out_proj_residual",
    )(a, w, resid)


def _rglru_kernel(ax_ref, ag_ref, cw_ref, cb_ref, wg_ref, brg_ref, big_ref, lam_ref, o_ref,
                  tail, a_s, u_s, h_s, state, *, n_blocks, block):
    ts = ax_ref.shape[0]

    @pl.when(pl.program_id(1) == 0)
    def _():
        tail[...] = jnp.zeros_like(tail)
        state[...] = jnp.zeros_like(state)

    for n in range(n_blocks):
        cols = slice(n * block, (n + 1) * block)
        x = ax_ref[:, cols].astype(F32)
        xc = cb_ref[:, cols] + _causal_taps(
            x, tail[:, cols], [cw_ref[j:j + 1, cols] for j in range(LRU_CONV)])
        tail[:, cols] = x[ts - SUBLANES:, :]
        g = jnp.dot(xc.astype(BF16), wg_ref[n], preferred_element_type=F32)
        r = _sigmoid(g[:, :block] + brg_ref[:, cols])
        i = _sigmoid(g[:, block:] + big_ref[:, cols])
        lam = lam_ref[:, cols]
        log_sig = jnp.minimum(lam, 0.0) - jnp.log1p(jnp.exp(-jnp.abs(lam)))
        a = jnp.exp2(r * (log_sig * (LRU_C * LOG2_E)))
        a_s[:, cols] = a
        u_s[:, cols] = xc * i * jnp.sqrt(_one_minus_exp(r * (log_sig * (-2.0 * LRU_C)), a * a))

    def group(gi, h):
        base = pl.multiple_of(gi * SUBLANES, SUBLANES)
        for r8 in range(SUBLANES):
            h = a_s[pl.ds(base + r8, 1), :] * h + u_s[pl.ds(base + r8, 1), :]
            h_s[pl.ds(base + r8, 1), :] = h
        return h

    state[...] = lax.fori_loop(0, ts // SUBLANES, group, state[...])
    o_ref[...] = (h_s[...] * _silu(ag_ref[...].astype(F32))).astype(o_ref.dtype)


def _rglru(p, conv_w, conv_b, w_gates, b_rg, b_ig, lru_lambda, batch, seq, width, ts=256):
    n_blocks, block = w_gates.shape[0], w_gates.shape[1]
    ts = min(ts, seq)
    nt = seq // ts
    row = lambda b, t: b * nt + t
    vec = pl.BlockSpec((1, width), lambda b, t: (0, 0))
    return pl.pallas_call(
        functools.partial(_rglru_kernel, n_blocks=n_blocks, block=block),
        grid=(batch, nt),
        in_specs=[pl.BlockSpec((ts, width), lambda b, t: (row(b, t), 0)),
                  pl.BlockSpec((ts, width), lambda b, t: (row(b, t), 1)),
                  pl.BlockSpec((LRU_CONV, width), lambda b, t: (0, 0)),
                  vec,
                  pl.BlockSpec((n_blocks, block, 2 * block), lambda b, t: (0, 0, 0)),
                  vec, vec, vec],
        out_specs=pl.BlockSpec((ts, width), lambda b, t: (row(b, t), 0)),
        out_shape=jax.ShapeDtypeStruct((batch * seq, width), BF16),
        scratch_shapes=[pltpu.VMEM((SUBLANES, width), F32),
                        pltpu.VMEM((ts, width), F32),
                        pltpu.VMEM((ts, width), F32),
                        pltpu.VMEM((ts, width), F32),
                        pltpu.VMEM((1, width), F32)],
        compiler_params=_cparams(("parallel", "arbitrary")),
        name="rglru",
    )(p, p, conv_w, conv_b.reshape(1, width), w_gates, b_rg.reshape(1, width),
      b_ig.reshape(1, width), lru_lambda.reshape(1, width))


def _diff_attn_kernel(q_ref, k_ref, v_ref, g_ref, lam_ref, sw_ref, o_ref, m_s, l_s, acc_s, *,
                      group, n_q_blocks):
    qi = pl.program_id(2)
    tq = q_ref.shape[0]
    n_blk = tq // LANES

    peel_first = group >= n_q_blocks
    if not peel_first:
        m_s[...] = jnp.full_like(m_s, MASK_VALUE)
        l_s[...] = jnp.zeros_like(l_s)
        acc_s[...] = jnp.zeros_like(acc_s)

    def chunk(j, diagonal, first=False):
        k0 = pl.multiple_of(j * tq, tq)
        v = v_ref[pl.ds(k0, tq), :]
        if diagonal:
            lower = (lax.broadcasted_iota(jnp.int32, (LANES, LANES), 1)
                     <= lax.broadcasted_iota(jnp.int32, (LANES, LANES), 0))
        for c in range(2):
            hd = slice(c * HEAD_DIM, (c + 1) * HEAD_DIM)
            s = lax.dot_general(q_ref[:, hd], k_ref[pl.ds(k0, tq), hd], (((1,), (1,)), ((), ())),
                                preferred_element_type=F32)
            m_prev = None if first else m_s[c]
            p_rows, l_rows, m_rows, alpha_rows = [], [], [], []
            for rb in range(n_blk):
                rows = slice(rb * LANES, (rb + 1) * LANES)
                live = range(rb + 1) if diagonal else range(n_blk)
                blocks = [s[rows, n * LANES:(n + 1) * LANES] for n in live]
                if diagonal:
                    blocks[rb] = jnp.where(lower, blocks[rb], MASK_VALUE)
                m_cur = blocks[0]
                for blk in blocks[1:]:
                    m_cur = jnp.maximum(m_cur, blk)
                m_cur = jnp.max(m_cur, axis=-1, keepdims=True)
                if first:
                    m_new = jnp.broadcast_to(m_cur, (LANES, LANES))
                    alpha = None
                else:
                    m_new = jnp.maximum(m_prev[rows], m_cur)
                    alpha = jnp.exp2(m_prev[rows] - m_new)
                parts = [jnp.exp2(blk - m_new) for blk in blocks]
                l_new = parts[0] if first else alpha * l_s[c, rows, :] + parts[0]
                for part in parts[1:]:
                    l_new = l_new + part
                parts += [jnp.zeros((LANES, LANES), F32)] * (n_blk - len(parts))
                p_rows.append(jnp.concatenate(parts, axis=-1).astype(BF16))
                l_rows.append(l_new)
                m_rows.append(m_new)
                alpha_rows.append(alpha)
            l_s[c] = jnp.concatenate(l_rows, axis=0)
            m_s[c] = jnp.concatenate(m_rows, axis=0)
            pv = jnp.dot(jnp.concatenate(p_rows, axis=0), v, preferred_element_type=F32)
            if first:
                acc_s[c] = pv
            else:
                alpha = jnp.concatenate(alpha_rows, axis=0)
                alpha_wide = jnp.concatenate([alpha] * (VALUE_DIM // LANES), axis=-1)
                acc_s[c] = alpha_wide * acc_s[c] + pv

    def below_diagonal(u, carry):
        for t in range(group):
            chunk(group * u + t, diagonal=False)
        return carry

    if group < n_q_blocks:
        lax.fori_loop(0, qi // group, below_diagonal, 0)

    for left in range(group):
        @pl.when(qi % group == left)
        def _():
            for t in range(left):
                chunk(qi - left + t, diagonal=False, first=peel_first and t == 0)
            chunk(qi, diagonal=True, first=peel_first and left == 0)

    lam_init = lam_ref[4:5, 0:1]
    lam = (jnp.exp(jnp.sum(lam_ref[0:1, :] * lam_ref[1:2, :], axis=-1, keepdims=True))
           - jnp.exp(jnp.sum(lam_ref[2:3, :] * lam_ref[3:4, :], axis=-1, keepdims=True))
           + lam_init)
    l0 = jnp.sum(l_s[0], axis=-1, keepdims=True)
    l1 = jnp.sum(l_s[1], axis=-1, keepdims=True)
    o = acc_s[0] * (1.0 / l0) - acc_s[1] * (lam / l1)
    o = o * lax.rsqrt(jnp.mean(o * o, axis=-1, keepdims=True) + SUBLN_EPS)
    o = o * (sw_ref[...] * (1.0 - lam_init))
    o_ref[...] = (o * _silu(g_ref[...].astype(F32))).astype(o_ref.dtype)


def _diff_attention(qk, p, lam_rows, subln_w, batch, seq, heads, v_col, g_col, tq=512):
    tq = min(tq, seq)
    nq = seq // tq
    return pl.pallas_call(
        functools.partial(_diff_attn_kernel, group=min(GROUP, nq), n_q_blocks=nq),
        grid=(batch, heads, nq),
        in_specs=[pl.BlockSpec((tq, VALUE_DIM), lambda b, h, i: (b * nq + i, h)),
                  pl.BlockSpec((seq, VALUE_DIM), lambda b, h, i: (b, heads + h)),
                  pl.BlockSpec((seq, VALUE_DIM), lambda b, h, i: (b, v_col + h)),
                  pl.BlockSpec((tq, VALUE_DIM), lambda b, h, i: (b * nq + i, g_col + h)),
                  pl.BlockSpec((SUBLANES, HEAD_DIM), lambda b, h, i: (0, 0)),
                  pl.BlockSpec((1, VALUE_DIM), lambda b, h, i: (0, 0))],
        out_specs=pl.BlockSpec((tq, VALUE_DIM), lambda b, h, i: (b * nq + i, h)),
        out_shape=jax.ShapeDtypeStruct((batch * seq, heads * VALUE_DIM), BF16),
        scratch_shapes=[pltpu.VMEM((2, tq, LANES), F32),
                        pltpu.VMEM((2, tq, LANES), F32),
                        pltpu.VMEM((2, tq, VALUE_DIM), F32)],
        compiler_params=_cparams(("parallel", "parallel", "arbitrary")),
        name="diff_attention",
    )(qk, qk, p, p, lam_rows, subln_w.reshape(1, VALUE_DIM))


def _sconv_kernel(b_ref, c_ref, x_ref, g_ref, w_ref, o_ref, tail):
    ts = b_ref.shape[0]

    @pl.when(pl.program_id(1) == 0)
    def _():
        tail[...] = jnp.zeros_like(tail)

    z = c_ref[...].astype(F32) * x_ref[...].astype(F32)
    conv = _causal_taps(z, tail[...], [w_ref[j:j + 1, :] for j in range(SCONV_K)])
    tail[...] = z[ts - SUBLANES:, :]
    o_ref[...] = (b_ref[...].astype(F32) * conv * _silu(g_ref[...].astype(F32))).astype(o_ref.dtype)


def _short_conv(p, conv_w, batch, seq, width, first_col, ts=256):
    ts = min(ts, seq)
    nt = seq // ts
    spec = lambda c: pl.BlockSpec((ts, width), lambda b, t: (b * nt + t, first_col + c))
    return pl.pallas_call(
        _sconv_kernel,
        grid=(batch, nt),
        in_specs=[spec(0), spec(1), spec(2), spec(3),
                  pl.BlockSpec((SCONV_K, width), lambda b, t: (0, 0))],
        out_specs=pl.BlockSpec((ts, width), lambda b, t: (b * nt + t, 0)),
        out_shape=jax.ShapeDtypeStruct((batch * seq, width), BF16),
        scratch_shapes=[pltpu.VMEM((SUBLANES, width), F32)],
        compiler_params=_cparams(("parallel", "arbitrary")),
        name="short_conv",
    )(p, p, p, p, conv_w)


def _merge_kernel(ya_ref, yb_ref, yc_ref, wa_ref, wb_ref, wc_ref, ga_ref, gb_ref, gc_ref,
                  ba_ref, bb_ref, bc_ref, o_ref):
    out = None
    for y_ref, w_ref, gl_ref, b_ref in ((ya_ref, wa_ref, ga_ref, ba_ref),
                                        (yb_ref, wb_ref, gb_ref, bb_ref),
                                        (yc_ref, wc_ref, gc_ref, bc_ref)):
        gate = _sigmoid(gl_ref[...].astype(F32) + b_ref[...])
        term = gate * jnp.dot(y_ref[...], w_ref[...], preferred_element_type=F32)
        out = term if out is None else out + term
    o_ref[...] = out.astype(o_ref.dtype)


def _merge(ya, yb, yc, w_branch, p, gate_b, gl_col, tm=1024, tn=512):
    m, width = ya.shape
    d = w_branch.shape[1]
    tm, tn = min(tm, m), min(tn, d)
    y_spec = pl.BlockSpec((tm, width), lambda i, j: (i, 0))
    w_spec = lambda br: pl.BlockSpec((width, tn), lambda i, j: (br, j))
    g_spec = lambda br: pl.BlockSpec((tm, tn), lambda i, j: (i, (gl_col + br * d) // tn + j))
    b_spec = lambda br: pl.BlockSpec((None, 1, tn), lambda i, j: (br, 0, j))
    gate_b3 = gate_b.reshape(N_BRANCHES, 1, d)
    return pl.pallas_call(
        _merge_kernel,
        grid=(m // tm, d // tn),
        in_specs=[y_spec, y_spec, y_spec, w_spec(0), w_spec(1), w_spec(2),
                  g_spec(0), g_spec(1), g_spec(2), b_spec(0), b_spec(1), b_spec(2)],
        out_specs=pl.BlockSpec((tm, tn), lambda i, j: (i, j)),
        out_shape=jax.ShapeDtypeStruct((m, d), BF16),
        compiler_params=_cparams(("parallel", "parallel")),
        name="gated_merge",
    )(ya, yb, yc, w_branch, w_branch, w_branch, p, p, p, gate_b3, gate_b3, gate_b3)


def kernel(x, positions, norm_w, w_in, gate_b, conv_a_w, conv_a_b, w_rg, b_rg, w_ig, b_ig,
           lru_lambda, lam_q1, lam_k1, lam_q2, lam_k2, subln_w, conv_c_w, w_branch, w_out,
           final_norm_w):
    batch, seq, d = x.shape
    depth = norm_w.shape[0]
    bw = w_branch.shape[2]
    heads = bw // VALUE_DIM
    m = batch * seq
    assert w_in.shape[2] == N_BRANCH_SLICES * bw + N_BRANCHES * d
    assert seq % SUBLANES == 0 and bw % VALUE_DIM == 0

    w_branch_rows = w_branch.reshape(depth, N_BRANCHES * bw, d)
    w_in_l = w_in[0].astype(BF16)
    w_branch_l = w_branch_rows[0].astype(BF16)
    w_out_l = w_out[0].astype(BF16)
    w_gates_bf = jnp.concatenate([w_rg, w_ig], axis=-1).astype(BF16)

    cos, sin = _rope_tables(positions)
    xf = x.reshape(m, d)
    for l in range(depth):
        lam_init = 0.8 - 0.6 * math.exp(-0.3 * l)
        lam_rows = jnp.concatenate(
            [lam_q1[l][None], lam_k1[l][None], lam_q2[l][None], lam_k2[l][None],
             jnp.full((1, HEAD_DIM), lam_init, F32), jnp.zeros((SUBLANES - 5, HEAD_DIM), F32)], axis=0)

        h = _rmsnorm(xf, norm_w[l], BF16)
        last = l == depth - 1
        next_srcs = [] if last else [(w_in, CAST_TILE_W_IN), (w_branch_rows, CAST_TILE_SMALL),
                                     (w_out, CAST_TILE_SMALL)]
        p, next_w = _in_proj(h, w_in_l, skip_start=2 * bw, skip_cols=2 * bw,
                             cast_srcs=next_srcs, cast_layer=l + 1)
        qk = _qk_proj_rope(h, w_in_l, col_start=2 * bw, q_cols=bw, cos=cos, sin=sin)

        ya = _rglru(p, conv_a_w[l], conv_a_b[l], w_gates_bf[l], b_rg[l], b_ig[l], lru_lambda[l],
                    batch, seq, bw)
        yb = _diff_attention(qk, p, lam_rows, subln_w[l], batch, seq, heads,
                             v_col=2 * heads, g_col=3 * heads)
        yc = _short_conv(p, conv_c_w[l], batch, seq, bw, first_col=4)
        merged = _merge(ya, yb, yc, w_branch_l, p, gate_b[l], gl_col=8 * bw)
        xf = _out_proj_residual(merged, w_out_l, xf)
        if not last:
            w_in_l, w_branch_l, w_out_l = next_w
    return _rmsnorm(xf, final_norm_w, F32).reshape(batch, seq, d)
```

```python
import functools
import math
from typing import NamedTuple

import jax
import jax.numpy as jnp
from jax import lax
from jax.experimental import pallas as pl
from jax.experimental.pallas import tpu as pltpu

F32 = jnp.float32
BF16 = jnp.bfloat16

HEAD_DIM = 128
VALUE_DIM = 2 * HEAD_DIM
LRU_C = 8.0
LRU_CONV = 4
SCONV_K = 3
ROPE_THETA = 10000.0
NORM_EPS = 1e-6
SUBLN_EPS = 1e-5
N_BRANCHES = 3
N_BRANCH_SLICES = 10

V7X_VMEM_BYTES = 64 * 1024 * 1024
VMEM_LIMIT = V7X_VMEM_BYTES - 8 * 1024 * 1024
SUBLANES = 8
LANES = 128
MASK_VALUE = -1e30
LOG2_E = 1.4426950408889634
GROUP = 8
CAST_TILE_W_IN = (512, 1024)
CAST_TILE_SMALL = (256, 1024)


def _cparams(semantics):
    return pltpu.CompilerParams(dimension_semantics=semantics, vmem_limit_bytes=VMEM_LIMIT)


def _sigmoid(x):
    return 1.0 / (1.0 + jnp.exp(-x))


def _silu(x):
    return x * _sigmoid(x)


def _one_minus_exp(z, exp_neg_z):
    poly = 1.0 - (z * 0.5) * (1.0 - z * (1.0 / 3.0))
    return jnp.where(z < 1.0 / 64.0, z * poly, 1.0 - exp_neg_z)


def _causal_taps(x, prev_tail, conv_w_rows):
    k_width = len(conv_w_rows)
    out = conv_w_rows[k_width - 1] * x
    head_rows = lax.broadcasted_iota(jnp.int32, (SUBLANES, x.shape[1]), 0)
    for shift in range(1, k_width):
        rolled = pltpu.roll(x, shift, axis=0)
        head = jnp.where(head_rows < shift, pltpu.roll(prev_tail, shift, axis=0), rolled[0:SUBLANES])
        shifted = jnp.concatenate([head, rolled[SUBLANES:]], axis=0)
        out = out + conv_w_rows[k_width - 1 - shift] * shifted
    return out


def _rmsnorm_kernel(x_ref, w_ref, o_ref, *, eps):
    x = x_ref[...].astype(F32)
    ms = jnp.mean(x * x, axis=-1, keepdims=True)
    o_ref[...] = (x * lax.rsqrt(ms + eps) * w_ref[...]).astype(o_ref.dtype)


def _rmsnorm(x, w, out_dtype, eps=NORM_EPS):
    m, d = x.shape
    tr = min(512, m)
    return pl.pallas_call(
        functools.partial(_rmsnorm_kernel, eps=eps),
        grid=(m // tr,),
        in_specs=[pl.BlockSpec((tr, d), lambda i: (i, 0)),
                  pl.BlockSpec((1, d), lambda i: (0, 0))],
        out_specs=pl.BlockSpec((tr, d), lambda i: (i, 0)),
        out_shape=jax.ShapeDtypeStruct((m, d), out_dtype),
        compiler_params=_cparams(("parallel",)),
        name="rmsnorm",
    )(x, w.reshape(1, d))


def _rope_table_kernel(pos_ref, invf_ref, sign_ref, cos_ref, sin_ref):
    ang = pos_ref[...].astype(F32) * invf_ref[...]
    cos_ref[...] = jnp.cos(ang)
    sin_ref[...] = jnp.sin(ang) * sign_ref[...]


def _rope_tables(positions):
    m = positions.size
    half = HEAD_DIM // 2
    inv_freq = ROPE_THETA ** (-jnp.arange(0, half, dtype=F32) * (2.0 / HEAD_DIM))
    invf = jnp.concatenate([inv_freq, inv_freq]).reshape(1, HEAD_DIM)
    sign = jnp.concatenate([-jnp.ones((half,), F32), jnp.ones((half,), F32)]).reshape(1, HEAD_DIM)
    tr = min(512, m)
    return pl.pallas_call(
        _rope_table_kernel,
        grid=(m // tr,),
        in_specs=[pl.BlockSpec((tr, 1), lambda i: (i, 0)),
                  pl.BlockSpec((1, HEAD_DIM), lambda i: (0, 0)),
                  pl.BlockSpec((1, HEAD_DIM), lambda i: (0, 0))],
        out_specs=[pl.BlockSpec((tr, HEAD_DIM), lambda i: (i, 0)),
                   pl.BlockSpec((tr, HEAD_DIM), lambda i: (i, 0))],
        out_shape=[jax.ShapeDtypeStruct((m, HEAD_DIM), F32)] * 2,
        compiler_params=_cparams(("parallel",)),
        name="rope_tables",
    )(positions.reshape(m, 1), invf, sign)


class _CastJob(NamedTuple):
    src: jax.Array
    layer: int
    first_step: int
    tile: tuple
    n_col_tiles: int
    n_tiles: int


def _plan_cast_jobs(srcs, layer, n_steps):
    jobs, rest, step = [], [], 0
    for src, tile in srcs:
        _, rows, cols = src.shape
        tr, tc = min(tile[0], rows), min(tile[1], cols)
        n_tiles = (rows // tr) * (cols // tc)
        if rows % tr or cols % tc or step + n_tiles > n_steps:
            rest.append(src)
            continue
        jobs.append(_CastJob(src, layer, step, (tr, tc), cols // tc, n_tiles))
        step += n_tiles
    return jobs, rest


def _cast_job_specs(job, n_j):
    def tile_index(i, j):
        t = jnp.clip(i * n_j + j - job.first_step, 0, job.n_tiles - 1)
        return t // job.n_col_tiles, t % job.n_col_tiles
    in_spec = pl.BlockSpec((None,) + job.tile, lambda i, j: (job.layer,) + tile_index(i, j))
    out_spec = pl.BlockSpec(job.tile, tile_index)
    return in_spec, out_spec, jax.ShapeDtypeStruct(job.src.shape[1:], BF16)


def _run_cast_jobs(jobs, src_refs, dst_refs):
    step = pl.program_id(0) * pl.num_programs(1) + pl.program_id(1)
    for job, src_ref, dst_ref in zip(jobs, src_refs, dst_refs):
        @pl.when((step >= job.first_step) & (step < job.first_step + job.n_tiles))
        def _():
            dst_ref[...] = src_ref[...].astype(BF16)


def _in_proj_kernel(x_ref, w_ref, *refs, jobs):
    o_ref = refs[len(jobs)]
    o_ref[...] = jnp.dot(x_ref[...], w_ref[...], preferred_element_type=F32).astype(o_ref.dtype)
    _run_cast_jobs(jobs, refs[:len(jobs)], refs[len(jobs) + 1:])


def _in_proj(x, w, skip_start, skip_cols, cast_srcs, cast_layer, tm=1024, tn=1024):
    m, k = x.shape
    n = w.shape[1] - skip_cols
    tm, tn = min(tm, m), min(tn, n, skip_cols)
    first, skip = skip_start // tn, skip_cols // tn
    grid = (m // tm, n // tn)
    jobs, rest = _plan_cast_jobs(cast_srcs, cast_layer, grid[0] * grid[1])
    specs = [_cast_job_specs(job, grid[1]) for job in jobs]
    outs = pl.pallas_call(
        functools.partial(_in_proj_kernel, jobs=jobs),
        grid=grid,
        in_specs=[pl.BlockSpec((tm, k), lambda i, j: (i, 0)),
                  pl.BlockSpec((k, tn), lambda i, j: (0, jnp.where(j >= first, j + skip, j)))]
                 + [s[0] for s in specs],
        out_specs=[pl.BlockSpec((tm, tn), lambda i, j: (i, j))] + [s[1] for s in specs],
        out_shape=[jax.ShapeDtypeStruct((m, n), BF16)] + [s[2] for s in specs],
        compiler_params=_cparams(("arbitrary", "arbitrary")),
        name="in_proj",
    )(x, w, *[job.src for job in jobs])
    by_src = {id(job.src): out for job, out in zip(jobs, outs[1:])}
    by_src.update({id(src): src[cast_layer].astype(BF16) for src in rest})
    return outs[0], [by_src[id(src)] for src, _ in cast_srcs]


def _matmul_rope_kernel(x_ref, w_ref, cos_ref, sin_ref, o_ref, *, q_col_blocks, q_scale):
    acc = jnp.dot(x_ref[...], w_ref[...], preferred_element_type=F32)
    scale = jnp.where(pl.program_id(1) < q_col_blocks, q_scale, 1.0).astype(F32)
    cos = cos_ref[...] * scale
    sin = sin_ref[...] * scale
    for c in range(acc.shape[1] // HEAD_DIM):
        t = acc[:, c * HEAD_DIM:(c + 1) * HEAD_DIM]
        swapped = pltpu.roll(t, HEAD_DIM // 2, axis=1)
        o_ref[:, c * HEAD_DIM:(c + 1) * HEAD_DIM] = (t * cos + swapped * sin).astype(o_ref.dtype)


def _qk_proj_rope(x, w, col_start, q_cols, cos, sin, tm=1024, tn=1024):
    m, k = x.shape
    n = 2 * q_cols
    tm, tn = min(tm, m), min(tn, q_cols)
    first = col_start // tn
    return pl.pallas_call(
        functools.partial(_matmul_rope_kernel, q_col_blocks=q_cols // tn,
                          q_scale=HEAD_DIM ** -0.5 * LOG2_E),
        grid=(m // tm, n // tn),
        in_specs=[pl.BlockSpec((tm, k), lambda i, j: (i, 0)),
                  pl.BlockSpec((k, tn), lambda i, j: (0, first + j)),
                  pl.BlockSpec((tm, HEAD_DIM), lambda i, j: (i, 0)),
                  pl.BlockSpec((tm, HEAD_DIM), lambda i, j: (i, 0))],
        out_specs=pl.BlockSpec((tm, tn), lambda i, j: (i, j)),
        out_shape=jax.ShapeDtypeStruct((m, n), BF16),
        compiler_params=_cparams(("parallel", "parallel")),
        name="qk_proj_rope",
    )(x, w, cos, sin)


def _matmul_residual_kernel(a_ref, w_ref, r_ref, o_ref):
    o_ref[...] = r_ref[...] + jnp.dot(a_ref[...], w_ref[...], preferred_element_type=F32)


def _out_proj_residual(a, w, resid, tm=1024, tn=1024):
    m, k = a.shape
    n = w.shape[1]
    tm, tn = min(tm, m), min(tn, n)
    return pl.pallas_call(
        _matmul_residual_kernel,
        grid=(m // tm, n // tn),
        in_specs=[pl.BlockSpec((tm, k), lambda i, j: (i, 0)),
                  pl.BlockSpec((k, tn), lambda i, j: (0, j)),
                  pl.BlockSpec((tm, tn), lambda i, j: (i, j))],
        out_specs=pl.BlockSpec((tm, tn), lambda i, j: (i, j)),
        out_shape=jax.ShapeDtypeStruct((m, n), F32),
        compiler_params=_cparams(("parallel", "parallel")),
        name="out_proj_residual",
    )(a, w, resid)


def _rglru_kernel(ax_ref, ag_ref, cw_ref, cb_ref, wg_ref, brg_ref, big_ref, lam_ref, o_ref,
                  tail, a_s, u_s, h_s, state, *, n_blocks, block):
    ts = ax_ref.shape[0]

    @pl.when(pl.program_id(1) == 0)
    def _():
        tail[...] = jnp.zeros_like(tail)
        state[...] = jnp.zeros_like(state)

    for n in range(n_blocks):
        cols = slice(n * block, (n + 1) * block)
        x = ax_ref[:, cols].astype(F32)
        xc = cb_ref[:, cols] + _causal_taps(
            x, tail[:, cols], [cw_ref[j:j + 1, cols] for j in range(LRU_CONV)])
        tail[:, cols] = x[ts - SUBLANES:, :]
        g = jnp.dot(xc.astype(BF16), wg_ref[n], preferred_element_type=F32)
        r = _sigmoid(g[:, :block] + brg_ref[:, cols])
        i = _sigmoid(g[:, block:] + big_ref[:, cols])
        lam = lam_ref[:, cols]
        log_sig = jnp.minimum(lam, 0.0) - jnp.log1p(jnp.exp(-jnp.abs(lam)))
        a = jnp.exp2(r * (log_sig * (LRU_C * LOG2_E)))
        a_s[:, cols] = a
        u_s[:, cols] = xc * i * jnp.sqrt(_one_minus_exp(r * (log_sig * (-2.0 * LRU_C)), a * a))

    def group(gi, h):
        base = pl.multiple_of(gi * SUBLANES, SUBLANES)
        for r8 in range(SUBLANES):
            h = a_s[pl.ds(base + r8, 1), :] * h + u_s[pl.ds(base + r8, 1), :]
            h_s[pl.ds(base + r8, 1), :] = h
        return h

    state[...] = lax.fori_loop(0, ts // SUBLANES, group, state[...])
    o_ref[...] = (h_s[...] * _silu(ag_ref[...].astype(F32))).astype(o_ref.dtype)


def _rglru(p, conv_w, conv_b, w_gates, b_rg, b_ig, lru_lambda, batch, seq, width, ts=256):
    n_blocks, block = w_gates.shape[0], w_gates.shape[1]
    ts = min(ts, seq)
    nt = seq // ts
    row = lambda b, t: b * nt + t
    vec = pl.BlockSpec((1, width), lambda b, t: (0, 0))
    return pl.pallas_call(
        functools.partial(_rglru_kernel, n_blocks=n_blocks, block=block),
        grid=(batch, nt),
        in_specs=[pl.BlockSpec((ts, width), lambda b, t: (row(b, t), 0)),
                  pl.BlockSpec((ts, width), lambda b, t: (row(b, t), 1)),
                  pl.BlockSpec((LRU_CONV, width), lambda b, t: (0, 0)),
                  vec,
                  pl.BlockSpec((n_blocks, block, 2 * block), lambda b, t: (0, 0, 0)),
                  vec, vec, vec],
        out_specs=pl.BlockSpec((ts, width), lambda b, t: (row(b, t), 0)),
        out_shape=jax.ShapeDtypeStruct((batch * seq, width), BF16),
        scratch_shapes=[pltpu.VMEM((SUBLANES, width), F32),
                        pltpu.VMEM((ts, width), F32),
                        pltpu.VMEM((ts, width), F32),
                        pltpu.VMEM((ts, width), F32),
                        pltpu.VMEM((1, width), F32)],
        compiler_params=_cparams(("parallel", "arbitrary")),
        name="rglru",
    )(p, p, conv_w, conv_b.reshape(1, width), w_gates, b_rg.reshape(1, width),
      b_ig.reshape(1, width), lru_lambda.reshape(1, width))


def _diff_attn_kernel(q_ref, k_ref, v_ref, g_ref, lam_ref, sw_ref, o_ref, m_s, l_s, acc_s, *,
                      group, n_q_blocks):
    qi = pl.program_id(2)
    tq = q_ref.shape[0]
    n_blk = tq // LANES

    peel_first = group >= n_q_blocks
    if not peel_first:
        m_s[...] = jnp.full_like(m_s, MASK_VALUE)
        l_s[...] = jnp.zeros_like(l_s)
        acc_s[...] = jnp.zeros_like(acc_s)

    def chunk(j, diagonal, first=False):
        k0 = pl.multiple_of(j * tq, tq)
        v = v_ref[pl.ds(k0, tq), :]
        if diagonal:
            lower = (lax.broadcasted_iota(jnp.int32, (LANES, LANES), 1)
                     <= lax.broadcasted_iota(jnp.int32, (LANES, LANES), 0))
        for c in range(2):
            hd = slice(c * HEAD_DIM, (c + 1) * HEAD_DIM)
            s = lax.dot_general(q_ref[:, hd], k_ref[pl.ds(k0, tq), hd], (((1,), (1,)), ((), ())),
                                preferred_element_type=F32)
            m_prev = None if first else m_s[c]
            p_rows, l_rows, m_rows, alpha_rows = [], [], [], []
            for rb in range(n_blk):
                rows = slice(rb * LANES, (rb + 1) * LANES)
                live = range(rb + 1) if diagonal else range(n_blk)
                blocks = [s[rows, n * LANES:(n + 1) * LANES] for n in live]
                if diagonal:
                    blocks[rb] = jnp.where(lower, blocks[rb], MASK_VALUE)
                m_cur = blocks[0]
                for blk in blocks[1:]:
                    m_cur = jnp.maximum(m_cur, blk)
                m_cur = jnp.max(m_cur, axis=-1, keepdims=True)
                if first:
                    m_new = jnp.broadcast_to(m_cur, (LANES, LANES))
                    alpha = None
                else:
                    m_new = jnp.maximum(m_prev[rows], m_cur)
                    alpha = jnp.exp2(m_prev[rows] - m_new)
                parts = [jnp.exp2(blk - m_new) for blk in blocks]
                l_new = parts[0] if first else alpha * l_s[c, rows, :] + parts[0]
                for part in parts[1:]:
                    l_new = l_new + part
                parts += [jnp.zeros((LANES, LANES), F32)] * (n_blk - len(parts))
                p_rows.append(jnp.concatenate(parts, axis=-1).astype(BF16))
                l_rows.append(l_new)
                m_rows.append(m_new)
                alpha_rows.append(alpha)
            l_s[c] = jnp.concatenate(l_rows, axis=0)
            m_s[c] = jnp.concatenate(m_rows, axis=0)
            pv = jnp.dot(jnp.concatenate(p_rows, axis=0), v, preferred_element_type=F32)
            if first:
                acc_s[c] = pv
            else:
                alpha = jnp.concatenate(alpha_rows, axis=0)
                alpha_wide = jnp.concatenate([alpha] * (VALUE_DIM // LANES), axis=-1)
                acc_s[c] = alpha_wide * acc_s[c] + pv

    def finish():
        lam_init = lam_ref[4:5, 0:1]
        lam = (jnp.exp(jnp.sum(lam_ref[0:1, :] * lam_ref[1:2, :], axis=-1, keepdims=True))
               - jnp.exp(jnp.sum(lam_ref[2:3, :] * lam_ref[3:4, :], axis=-1, keepdims=True))
               + lam_init)
        l0 = jnp.sum(l_s[0], axis=-1, keepdims=True)
        l1 = jnp.sum(l_s[1], axis=-1, keepdims=True)
        o = acc_s[0] * (1.0 / l0) - acc_s[1] * (lam / l1)
        o = o * lax.rsqrt(jnp.mean(o * o, axis=-1, keepdims=True) + SUBLN_EPS)
        o = o * (sw_ref[...] * (1.0 - lam_init))
        o_ref[...] = (o * _silu(g_ref[...].astype(F32))).astype(o_ref.dtype)

    def below_diagonal(u, carry):
        for t in range(group):
            chunk(group * u + t, diagonal=False)
        return carry

    if group < n_q_blocks:
        lax.fori_loop(0, qi // group, below_diagonal, 0)

    for left in range(group):
        @pl.when(qi % group == left)
        def _():
            for t in range(left):
                chunk(qi - left + t, diagonal=False, first=peel_first and t == 0)
            chunk(qi, diagonal=True, first=peel_first and left == 0)
            finish()


def _diff_attention(qk, p, lam_rows, subln_w, batch, seq, heads, v_col, g_col, tq=512):
    tq = min(tq, seq)
    nq = seq // tq
    return pl.pallas_call(
        functools.partial(_diff_attn_kernel, group=min(GROUP, nq), n_q_blocks=nq),
        grid=(batch, heads, nq),
        in_specs=[pl.BlockSpec((tq, VALUE_DIM), lambda b, h, i: (b * nq + i, h)),
                  pl.BlockSpec((seq, VALUE_DIM), lambda b, h, i: (b, heads + h)),
                  pl.BlockSpec((seq, VALUE_DIM), lambda b, h, i: (b, v_col + h)),
                  pl.BlockSpec((tq, VALUE_DIM), lambda b, h, i: (b * nq + i, g_col + h)),
                  pl.BlockSpec((SUBLANES, HEAD_DIM), lambda b, h, i: (0, 0)),
                  pl.BlockSpec((1, VALUE_DIM), lambda b, h, i: (0, 0))],
        out_specs=pl.BlockSpec((tq, VALUE_DIM), lambda b, h, i: (b * nq + i, h)),
        out_shape=jax.ShapeDtypeStruct((batch * seq, heads * VALUE_DIM), BF16),
        scratch_shapes=[pltpu.VMEM((2, tq, LANES), F32),
                        pltpu.VMEM((2, tq, LANES), F32),
                        pltpu.VMEM((2, tq, VALUE_DIM), F32)],
        compiler_params=_cparams(("parallel", "parallel", "arbitrary")),
        name="diff_attention",
    )(qk, qk, p, p, lam_rows, subln_w.reshape(1, VALUE_DIM))


def _sconv_kernel(b_ref, c_ref, x_ref, g_ref, w_ref, o_ref, tail):
    ts = b_ref.shape[0]

    @pl.when(pl.program_id(1) == 0)
    def _():
        tail[...] = jnp.zeros_like(tail)

    z = c_ref[...].astype(F32) * x_ref[...].astype(F32)
    conv = _causal_taps(z, tail[...], [w_ref[j:j + 1, :] for j in range(SCONV_K)])
    tail[...] = z[ts - SUBLANES:, :]
    o_ref[...] = (b_ref[...].astype(F32) * conv * _silu(g_ref[...].astype(F32))).astype(o_ref.dtype)


def _short_conv(p, conv_w, batch, seq, width, first_col, ts=256):
    ts = min(ts, seq)
    nt = seq // ts
    spec = lambda c: pl.BlockSpec((ts, width), lambda b, t: (b * nt + t, first_col + c))
    return pl.pallas_call(
        _sconv_kernel,
        grid=(batch, nt),
        in_specs=[spec(0), spec(1), spec(2), spec(3),
                  pl.BlockSpec((SCONV_K, width), lambda b, t: (0, 0))],
        out_specs=pl.BlockSpec((ts, width), lambda b, t: (b * nt + t, 0)),
        out_shape=jax.ShapeDtypeStruct((batch * seq, width), BF16),
        scratch_shapes=[pltpu.VMEM((SUBLANES, width), F32)],
        compiler_params=_cparams(("parallel", "arbitrary")),
        name="short_conv",
    )(p, p, p, p, conv_w)


def _merge_kernel(ya_ref, yb_ref, yc_ref, wa_ref, wb_ref, wc_ref, ga_ref, gb_ref, gc_ref,
                  ba_ref, bb_ref, bc_ref, o_ref):
    out = None
    for y_ref, w_ref, gl_ref, b_ref in ((ya_ref, wa_ref, ga_ref, ba_ref),
                                        (yb_ref, wb_ref, gb_ref, bb_ref),
                                        (yc_ref, wc_ref, gc_ref, bc_ref)):
        gate = _sigmoid(gl_ref[...].astype(F32) + b_ref[...])
        term = gate * jnp.dot(y_ref[...], w_ref[...], preferred_element_type=F32)
        out = term if out is None else out + term
    o_ref[...] = out.astype(o_ref.dtype)


def _merge(ya, yb, yc, w_branch, p, gate_b, gl_col, tm=1024, tn=512):
    m, width = ya.shape
    d = w_branch.shape[1]
    tm, tn = min(tm, m), min(tn, d)
    y_spec = pl.BlockSpec((tm, width), lambda i, j: (i, 0))
    w_spec = lambda br: pl.BlockSpec((width, tn), lambda i, j: (br, j))
    g_spec = lambda br: pl.BlockSpec((tm, tn), lambda i, j: (i, (gl_col + br * d) // tn + j))
    b_spec = lambda br: pl.BlockSpec((None, 1, tn), lambda i, j: (br, 0, j))
    gate_b3 = gate_b.reshape(N_BRANCHES, 1, d)
    return pl.pallas_call(
        _merge_kernel,
        grid=(m // tm, d // tn),
        in_specs=[y_spec, y_spec, y_spec, w_spec(0), w_spec(1), w_spec(2),
                  g_spec(0), g_spec(1), g_spec(2), b_spec(0), b_spec(1), b_spec(2)],
        out_specs=pl.BlockSpec((tm, tn), lambda i, j: (i, j)),
        out_shape=jax.ShapeDtypeStruct((m, d), BF16),
        compiler_params=_cparams(("parallel", "parallel")),
        name="gated_merge",
    )(ya, yb, yc, w_branch, w_branch, w_branch, p, p, p, gate_b3, gate_b3, gate_b3)


def kernel(x, positions, norm_w, w_in, gate_b, conv_a_w, conv_a_b, w_rg, b_rg, w_ig, b_ig,
           lru_lambda, lam_q1, lam_k1, lam_q2, lam_k2, subln_w, conv_c_w, w_branch, w_out,
           final_norm_w):
    batch, seq, d = x.shape
    depth = norm_w.shape[0]
    bw = w_branch.shape[2]
    heads = bw // VALUE_DIM
    m = batch * seq
    assert w_in.shape[2] == N_BRANCH_SLICES * bw + N_BRANCHES * d
    assert seq % SUBLANES == 0 and bw % VALUE_DIM == 0

    w_branch_rows = w_branch.reshape(depth, N_BRANCHES * bw, d)
    w_in_l = w_in[0].astype(BF16)
    w_branch_l = w_branch_rows[0].astype(BF16)
    w_out_l = w_out[0].astype(BF16)
    w_gates_bf = jnp.concatenate([w_rg, w_ig], axis=-1).astype(BF16)

    cos, sin = _rope_tables(positions)
    xf = x.reshape(m, d)
    for l in range(depth):
        lam_init = 0.8 - 0.6 * math.exp(-0.3 * l)
        lam_rows = jnp.concatenate(
            [lam_q1[l][None], lam_k1[l][None], lam_q2[l][None], lam_k2[l][None],
             jnp.full((1, HEAD_DIM), lam_init, F32), jnp.zeros((SUBLANES - 5, HEAD_DIM), F32)], axis=0)

        h = _rmsnorm(xf, norm_w[l], BF16)
        last = l == depth - 1
        next_srcs = [] if last else [(w_in, CAST_TILE_W_IN), (w_branch_rows, CAST_TILE_SMALL),
                                     (w_out, CAST_TILE_SMALL)]
        p, next_w = _in_proj(h, w_in_l, skip_start=2 * bw, skip_cols=2 * bw,
                             cast_srcs=next_srcs, cast_layer=l + 1)
        qk = _qk_proj_rope(h, w_in_l, col_start=2 * bw, q_cols=bw, cos=cos, sin=sin)

        ya = _rglru(p, conv_a_w[l], conv_a_b[l], w_gates_bf[l], b_rg[l], b_ig[l], lru_lambda[l],
                    batch, seq, bw)
        yb = _diff_attention(qk, p, lam_rows, subln_w[l], batch, seq, heads,
                             v_col=2 * heads, g_col=3 * heads)
        yc = _short_conv(p, conv_c_w[l], batch, seq, bw, first_col=4)
        merged = _merge(ya, yb, yc, w_branch_l, p, gate_b[l], gl_col=8 * bw)
        xf = _out_proj_residual(merged, w_out_l, xf)
        if not last:
            w_in_l, w_branch_l, w_out_l = next_w
    return _rmsnorm(xf, final_norm_w, F32).reshape(batch, seq, d)
```

```python
import functools
import math
from typing import NamedTuple

import jax
import jax.numpy as jnp
from jax import lax
from jax.experimental import pallas as pl
from jax.experimental.pallas import tpu as pltpu

F32 = jnp.float32
BF16 = jnp.bfloat16

HEAD_DIM = 128
VALUE_DIM = 2 * HEAD_DIM
LRU_C = 8.0
LRU_CONV = 4
SCONV_K = 3
ROPE_THETA = 10000.0
NORM_EPS = 1e-6
SUBLN_EPS = 1e-5
N_BRANCHES = 3
N_BRANCH_SLICES = 10

V7X_VMEM_BYTES = 64 * 1024 * 1024
VMEM_LIMIT = V7X_VMEM_BYTES - 4 * 1024 * 1024
SUBLANES = 8
LANES = 128
MASK_VALUE = -1e30
LOG2_E = 1.4426950408889634
GROUP = 8
CAST_TILE_W_IN = (512, 1024)
CAST_TILE_SMALL = (256, 1024)


def _cparams(semantics):
    return pltpu.CompilerParams(dimension_semantics=semantics, vmem_limit_bytes=VMEM_LIMIT)


def _sigmoid(x):
    return 1.0 / (1.0 + jnp.exp(-x))


def _silu(x):
    return x * _sigmoid(x)


def _one_minus_exp(z, exp_neg_z):
    poly = 1.0 - (z * 0.5) * (1.0 - z * (1.0 / 3.0))
    return jnp.where(z < 1.0 / 64.0, z * poly, 1.0 - exp_neg_z)


def _causal_taps(x, prev_tail, conv_w_rows):
    k_width = len(conv_w_rows)
    out = conv_w_rows[k_width - 1] * x
    head_rows = lax.broadcasted_iota(jnp.int32, (SUBLANES, x.shape[1]), 0)
    for shift in range(1, k_width):
        rolled = pltpu.roll(x, shift, axis=0)
        head = jnp.where(head_rows < shift, pltpu.roll(prev_tail, shift, axis=0), rolled[0:SUBLANES])
        shifted = jnp.concatenate([head, rolled[SUBLANES:]], axis=0)
        out = out + conv_w_rows[k_width - 1 - shift] * shifted
    return out


def _rmsnorm_kernel(x_ref, w_ref, o_ref, *, eps):
    x = x_ref[...].astype(F32)
    ms = jnp.mean(x * x, axis=-1, keepdims=True)
    o_ref[...] = (x * lax.rsqrt(ms + eps) * w_ref[...]).astype(o_ref.dtype)


def _rmsnorm(x, w, out_dtype, eps=NORM_EPS):
    m, d = x.shape
    tr = min(512, m)
    return pl.pallas_call(
        functools.partial(_rmsnorm_kernel, eps=eps),
        grid=(m // tr,),
        in_specs=[pl.BlockSpec((tr, d), lambda i: (i, 0)),
                  pl.BlockSpec((1, d), lambda i: (0, 0))],
        out_specs=pl.BlockSpec((tr, d), lambda i: (i, 0)),
        out_shape=jax.ShapeDtypeStruct((m, d), out_dtype),
        compiler_params=_cparams(("parallel",)),
        name="rmsnorm",
    )(x, w.reshape(1, d))


def _rope_table_kernel(pos_ref, invf_ref, sign_ref, cos_ref, sin_ref):
    ang = pos_ref[...].astype(F32) * invf_ref[...]
    cos_ref[...] = jnp.cos(ang)
    sin_ref[...] = jnp.sin(ang) * sign_ref[...]


def _rope_tables(positions):
    m = positions.size
    half = HEAD_DIM // 2
    inv_freq = ROPE_THETA ** (-jnp.arange(0, half, dtype=F32) * (2.0 / HEAD_DIM))
    invf = jnp.concatenate([inv_freq, inv_freq]).reshape(1, HEAD_DIM)
    sign = jnp.concatenate([-jnp.ones((half,), F32), jnp.ones((half,), F32)]).reshape(1, HEAD_DIM)
    tr = min(512, m)
    return pl.pallas_call(
        _rope_table_kernel,
        grid=(m // tr,),
        in_specs=[pl.BlockSpec((tr, 1), lambda i: (i, 0)),
                  pl.BlockSpec((1, HEAD_DIM), lambda i: (0, 0)),
                  pl.BlockSpec((1, HEAD_DIM), lambda i: (0, 0))],
        out_specs=[pl.BlockSpec((tr, HEAD_DIM), lambda i: (i, 0)),
                   pl.BlockSpec((tr, HEAD_DIM), lambda i: (i, 0))],
        out_shape=[jax.ShapeDtypeStruct((m, HEAD_DIM), F32)] * 2,
        compiler_params=_cparams(("parallel",)),
        name="rope_tables",
    )(positions.reshape(m, 1), invf, sign)


class _CastJob(NamedTuple):
    src: jax.Array
    layer: int
    first_step: int
    tile: tuple
    n_col_tiles: int
    n_tiles: int


def _plan_cast_jobs(srcs, layer, n_steps):
    jobs, rest, step = [], [], 0
    for src, tile in srcs:
        _, rows, cols = src.shape
        tr, tc = min(tile[0], rows), min(tile[1], cols)
        n_tiles = (rows // tr) * (cols // tc)
        if rows % tr or cols % tc or step + n_tiles > n_steps:
            rest.append(src)
            continue
        jobs.append(_CastJob(src, layer, step, (tr, tc), cols // tc, n_tiles))
        step += n_tiles
    return jobs, rest


def _cast_job_specs(job, n_j):
    def tile_index(i, j):
        t = jnp.clip(i * n_j + j - job.first_step, 0, job.n_tiles - 1)
        return t // job.n_col_tiles, t % job.n_col_tiles
    in_spec = pl.BlockSpec((None,) + job.tile, lambda i, j: (job.layer,) + tile_index(i, j))
    out_spec = pl.BlockSpec(job.tile, tile_index)
    return in_spec, out_spec, jax.ShapeDtypeStruct(job.src.shape[1:], BF16)


def _run_cast_jobs(jobs, src_refs, dst_refs):
    step = pl.program_id(0) * pl.num_programs(1) + pl.program_id(1)
    for job, src_ref, dst_ref in zip(jobs, src_refs, dst_refs):
        @pl.when((step >= job.first_step) & (step < job.first_step + job.n_tiles))
        def _():
            dst_ref[...] = src_ref[...].astype(BF16)


def _row_rms_scale(ssq_ref, d):
    return lax.rsqrt(jnp.sum(ssq_ref[...], axis=-1, keepdims=True) * (1.0 / d) + NORM_EPS)


def _scaled_bf16_and_ssq(x, w_row):
    ssq = x[:, 0:LANES] * x[:, 0:LANES]
    for c in range(LANES, x.shape[1], LANES):
        ssq = ssq + x[:, c:c + LANES] * x[:, c:c + LANES]
    return (x * w_row).astype(BF16), ssq


def _norm_prep_kernel(x_ref, w_ref, xw_ref, ssq_ref):
    xw_ref[...], ssq_ref[...] = _scaled_bf16_and_ssq(x_ref[...], w_ref[...])


def _norm_prep(x, w):
    m, d = x.shape
    tr = min(512, m)
    return pl.pallas_call(
        _norm_prep_kernel,
        grid=(m // tr,),
        in_specs=[pl.BlockSpec((tr, d), lambda i: (i, 0)),
                  pl.BlockSpec((1, d), lambda i: (0, 0))],
        out_specs=[pl.BlockSpec((tr, d), lambda i: (i, 0)),
                   pl.BlockSpec((tr, LANES), lambda i: (i, 0))],
        out_shape=[jax.ShapeDtypeStruct((m, d), BF16), jax.ShapeDtypeStruct((m, LANES), F32)],
        compiler_params=_cparams(("parallel",)),
        name="norm_prep",
    )(x, w.reshape(1, d))


def _in_proj_kernel(x_ref, ssq_ref, w_ref, *refs, jobs):
    o_ref = refs[len(jobs)]
    acc = jnp.dot(x_ref[...], w_ref[...], preferred_element_type=F32)
    o_ref[...] = (acc * _row_rms_scale(ssq_ref, x_ref.shape[1])).astype(o_ref.dtype)
    _run_cast_jobs(jobs, refs[:len(jobs)], refs[len(jobs) + 1:])


def _in_proj(x, ssq, w, skip_start, skip_cols, cast_srcs, cast_layer, tm=1024, tn=1024):
    m, k = x.shape
    n = w.shape[1] - skip_cols
    tm, tn = min(tm, m), min(tn, n, skip_cols)
    first, skip = skip_start // tn, skip_cols // tn
    grid = (m // tm, n // tn)
    jobs, rest = _plan_cast_jobs(cast_srcs, cast_layer, grid[0] * grid[1])
    specs = [_cast_job_specs(job, grid[1]) for job in jobs]
    outs = pl.pallas_call(
        functools.partial(_in_proj_kernel, jobs=jobs),
        grid=grid,
        in_specs=[pl.BlockSpec((tm, k), lambda i, j: (i, 0)),
                  pl.BlockSpec((tm, LANES), lambda i, j: (i, 0)),
                  pl.BlockSpec((k, tn), lambda i, j: (0, jnp.where(j >= first, j + skip, j)))]
                 + [s[0] for s in specs],
        out_specs=[pl.BlockSpec((tm, tn), lambda i, j: (i, j))] + [s[1] for s in specs],
        out_shape=[jax.ShapeDtypeStruct((m, n), BF16)] + [s[2] for s in specs],
        compiler_params=_cparams(("arbitrary", "arbitrary")),
        name="in_proj",
    )(x, ssq, w, *[job.src for job in jobs])
    by_src = {id(job.src): out for job, out in zip(jobs, outs[1:])}
    by_src.update({id(src): src[cast_layer].astype(BF16) for src in rest})
    return outs[0], [by_src[id(src)] for src, _ in cast_srcs]


def _matmul_rope_kernel(x_ref, ssq_ref, w_ref, cos_ref, sin_ref, o_ref, *, q_col_blocks, q_scale):
    acc = jnp.dot(x_ref[...], w_ref[...], preferred_element_type=F32)
    scale = jnp.where(pl.program_id(1) < q_col_blocks, q_scale, 1.0).astype(F32)
    row_scale = _row_rms_scale(ssq_ref, x_ref.shape[1]) * scale
    cos = cos_ref[...] * row_scale
    sin = sin_ref[...] * row_scale
    for c in range(acc.shape[1] // HEAD_DIM):
        t = acc[:, c * HEAD_DIM:(c + 1) * HEAD_DIM]
        swapped = pltpu.roll(t, HEAD_DIM // 2, axis=1)
        o_ref[:, c * HEAD_DIM:(c + 1) * HEAD_DIM] = (t * cos + swapped * sin).astype(o_ref.dtype)


def _qk_proj_rope(x, ssq, w, col_start, q_cols, cos, sin, tm=1024, tn=1024):
    m, k = x.shape
    n = 2 * q_cols
    tm, tn = min(tm, m), min(tn, q_cols)
    first = col_start // tn
    return pl.pallas_call(
        functools.partial(_matmul_rope_kernel, q_col_blocks=q_cols // tn,
                          q_scale=HEAD_DIM ** -0.5 * LOG2_E),
        grid=(m // tm, n // tn),
        in_specs=[pl.BlockSpec((tm, k), lambda i, j: (i, 0)),
                  pl.BlockSpec((tm, LANES), lambda i, j: (i, 0)),
                  pl.BlockSpec((k, tn), lambda i, j: (0, first + j)),
                  pl.BlockSpec((tm, HEAD_DIM), lambda i, j: (i, 0)),
                  pl.BlockSpec((tm, HEAD_DIM), lambda i, j: (i, 0))],
        out_specs=pl.BlockSpec((tm, tn), lambda i, j: (i, j)),
        out_shape=jax.ShapeDtypeStruct((m, n), BF16),
        compiler_params=_cparams(("parallel", "parallel")),
        name="qk_proj_rope",
    )(x, ssq, w, cos, sin)


def _matmul_residual_kernel(a_ref, w_ref, r_ref, o_ref):
    o_ref[...] = r_ref[...] + jnp.dot(a_ref[...], w_ref[...], preferred_element_type=F32)


def _matmul_residual_norm_kernel(a_ref, w_ref, r_ref, nw_ref, o_ref, xw_ref, ssq_ref):
    @pl.when(pl.program_id(1) == 0)
    def _():
        ssq_ref[...] = jnp.zeros_like(ssq_ref)

    x_new = r_ref[...] + jnp.dot(a_ref[...], w_ref[...], preferred_element_type=F32)
    o_ref[...] = x_new
    xw_ref[...], ssq = _scaled_bf16_and_ssq(x_new, nw_ref[...])
    ssq_ref[...] += ssq


def _out_proj_residual(a, w, resid, next_norm_w=None, tm=1024, tn=1024):
    m, k = a.shape
    n = w.shape[1]
    if next_norm_w is not None:
        tn //= 2
    tm, tn = min(tm, m), min(tn, n)
    tile = pl.BlockSpec((tm, tn), lambda i, j: (i, j))
    in_specs = [pl.BlockSpec((tm, k), lambda i, j: (i, 0)),
                pl.BlockSpec((k, tn), lambda i, j: (0, j)),
                tile]
    out_specs, out_shape, operands = [tile], [jax.ShapeDtypeStruct((m, n), F32)], [a, w, resid]
    if next_norm_w is not None:
        in_specs.append(pl.BlockSpec((1, tn), lambda i, j: (0, j)))
        operands.append(next_norm_w.reshape(1, n))
        out_specs += [tile, pl.BlockSpec((tm, LANES), lambda i, j: (i, 0))]
        out_shape += [jax.ShapeDtypeStruct((m, n), BF16), jax.ShapeDtypeStruct((m, LANES), F32)]
    outs = pl.pallas_call(
        _matmul_residual_kernel if next_norm_w is None else _matmul_residual_norm_kernel,
        grid=(m // tm, n // tn),
        in_specs=in_specs,
        out_specs=out_specs,
        out_shape=out_shape,
        compiler_params=_cparams(("parallel", "arbitrary")),
        name="out_proj_residual",
    )(*operands)
    return outs if next_norm_w is not None else outs[0]


def _rglru_kernel(ax_ref, ag_ref, cw_ref, cb_ref, wg_ref, brg_ref, big_ref, lam_ref, o_ref,
                  tail, a_s, u_s, h_s, state, *, n_blocks, block):
    ts = ax_ref.shape[0]

    @pl.when(pl.program_id(1) == 0)
    def _():
        tail[...] = jnp.zeros_like(tail)
        state[...] = jnp.zeros_like(state)

    for n in range(n_blocks):
        cols = slice(n * block, (n + 1) * block)
        x = ax_ref[:, cols].astype(F32)
        xc = cb_ref[:, cols] + _causal_taps(
            x, tail[:, cols], [cw_ref[j:j + 1, cols] for j in range(LRU_CONV)])
        tail[:, cols] = x[ts - SUBLANES:, :]
        g = jnp.dot(xc.astype(BF16), wg_ref[n], preferred_element_type=F32)
        r = _sigmoid(g[:, :block] + brg_ref[:, cols])
        i = _sigmoid(g[:, block:] + big_ref[:, cols])
        lam = lam_ref[:, cols]
        log_sig = jnp.minimum(lam, 0.0) - jnp.log1p(jnp.exp(-jnp.abs(lam)))
        a = jnp.exp2(r * (log_sig * (LRU_C * LOG2_E)))
        a_s[:, cols] = a
        u_s[:, cols] = xc * i * jnp.sqrt(_one_minus_exp(r * (log_sig * (-2.0 * LRU_C)), a * a))

    def group(gi, h):
        base = pl.multiple_of(gi * SUBLANES, SUBLANES)
        for r8 in range(SUBLANES):
            h = a_s[pl.ds(base + r8, 1), :] * h + u_s[pl.ds(base + r8, 1), :]
            h_s[pl.ds(base + r8, 1), :] = h
        return h

    state[...] = lax.fori_loop(0, ts // SUBLANES, group, state[...])
    o_ref[...] = (h_s[...] * _silu(ag_ref[...].astype(F32))).astype(o_ref.dtype)


def _rglru(p, conv_w, conv_b, w_gates, b_rg, b_ig, lru_lambda, batch, seq, width, ts=256):
    n_blocks, block = w_gates.shape[0], w_gates.shape[1]
    ts = min(ts, seq)
    nt = seq // ts
    row = lambda b, t: b * nt + t
    vec = pl.BlockSpec((1, width), lambda b, t: (0, 0))
    return pl.pallas_call(
        functools.partial(_rglru_kernel, n_blocks=n_blocks, block=block),
        grid=(batch, nt),
        in_specs=[pl.BlockSpec((ts, width), lambda b, t: (row(b, t), 0)),
                  pl.BlockSpec((ts, width), lambda b, t: (row(b, t), 1)),
                  pl.BlockSpec((LRU_CONV, width), lambda b, t: (0, 0)),
                  vec,
                  pl.BlockSpec((n_blocks, block, 2 * block), lambda b, t: (0, 0, 0)),
                  vec, vec, vec],
        out_specs=pl.BlockSpec((ts, width), lambda b, t: (row(b, t), 0)),
        out_shape=jax.ShapeDtypeStruct((batch * seq, width), BF16),
        scratch_shapes=[pltpu.VMEM((SUBLANES, width), F32),
                        pltpu.VMEM((ts, width), F32),
                        pltpu.VMEM((ts, width), F32),
                        pltpu.VMEM((ts, width), F32),
                        pltpu.VMEM((1, width), F32)],
        compiler_params=_cparams(("parallel", "arbitrary")),
        name="rglru",
    )(p, p, conv_w, conv_b.reshape(1, width), w_gates, b_rg.reshape(1, width),
      b_ig.reshape(1, width), lru_lambda.reshape(1, width))


def _diff_attn_kernel(q_ref, k_ref, v_ref, g_ref, lam_ref, sw_ref, o_ref, m_s, l_s, acc_s, *,
                      group, n_q_blocks):
    qi = pl.program_id(2)
    tq = q_ref.shape[0]
    n_blk = tq // LANES

    peel_first = group >= n_q_blocks
    if not peel_first:
        m_s[...] = jnp.full_like(m_s, MASK_VALUE)
        l_s[...] = jnp.zeros_like(l_s)
        acc_s[...] = jnp.zeros_like(acc_s)

    def chunk(j, diagonal, first=False):
        k0 = pl.multiple_of(j * tq, tq)
        v = v_ref[pl.ds(k0, tq), :]
        if diagonal:
            lower = (lax.broadcasted_iota(jnp.int32, (LANES, LANES), 1)
                     <= lax.broadcasted_iota(jnp.int32, (LANES, LANES), 0))
        for c in range(2):
            hd = slice(c * HEAD_DIM, (c + 1) * HEAD_DIM)
            s = lax.dot_general(q_ref[:, hd], k_ref[pl.ds(k0, tq), hd], (((1,), (1,)), ((), ())),
                                preferred_element_type=F32)
            m_prev = None if first else m_s[c]
            p_rows, l_rows, m_rows, alpha_rows = [], [], [], []
            for rb in range(n_blk):
                rows = slice(rb * LANES, (rb + 1) * LANES)
                live = range(rb + 1) if diagonal else range(n_blk)
                blocks = [s[rows, n * LANES:(n + 1) * LANES] for n in live]
                if diagonal:
                    blocks[rb] = jnp.where(lower, blocks[rb], MASK_VALUE)
                m_cur = blocks[0]
                for blk in blocks[1:]:
                    m_cur = jnp.maximum(m_cur, blk)
                m_cur = jnp.max(m_cur, axis=-1, keepdims=True)
                if first:
                    m_new = jnp.broadcast_to(m_cur, (LANES, LANES))
                    alpha = None
                else:
                    m_new = jnp.maximum(m_prev[rows], m_cur)
                    alpha = jnp.exp2(m_prev[rows] - m_new)
                parts = [jnp.exp2(blk - m_new) for blk in blocks]
                l_new = parts[0] if first else alpha * l_s[c, rows, :] + parts[0]
                for part in parts[1:]:
                    l_new = l_new + part
                parts += [jnp.zeros((LANES, LANES), F32)] * (n_blk - len(parts))
                p_rows.append(jnp.concatenate(parts, axis=-1).astype(BF16))
                l_rows.append(l_new)
                m_rows.append(m_new)
                alpha_rows.append(alpha)
            l_s[c] = jnp.concatenate(l_rows, axis=0)
            m_s[c] = jnp.concatenate(m_rows, axis=0)
            pv = jnp.dot(jnp.concatenate(p_rows, axis=0), v, preferred_element_type=F32)
            if first:
                acc_s[c] = pv
            else:
                alpha = jnp.concatenate(alpha_rows, axis=0)
                alpha_wide = jnp.concatenate([alpha] * (VALUE_DIM // LANES), axis=-1)
                acc_s[c] = alpha_wide * acc_s[c] + pv

    def finish():
        lam_init = lam_ref[4:5, 0:1]
        lam = (jnp.exp(jnp.sum(lam_ref[0:1, :] * lam_ref[1:2, :], axis=-1, keepdims=True))
               - jnp.exp(jnp.sum(lam_ref[2:3, :] * lam_ref[3:4, :], axis=-1, keepdims=True))
               + lam_init)
        l0 = jnp.sum(l_s[0], axis=-1, keepdims=True)
        l1 = jnp.sum(l_s[1], axis=-1, keepdims=True)
        o = acc_s[0] * (1.0 / l0) - acc_s[1] * (lam / l1)
        o = o * lax.rsqrt(jnp.mean(o * o, axis=-1, keepdims=True) + SUBLN_EPS)
        o = o * (sw_ref[...] * (1.0 - lam_init))
        o_ref[...] = (o * _silu(g_ref[...].astype(F32))).astype(o_ref.dtype)

    def below_diagonal(u, carry):
        for t in range(group):
            chunk(group * u + t, diagonal=False)
        return carry

    if group < n_q_blocks:
        lax.fori_loop(0, qi // group, below_diagonal, 0)

    for left in range(group):
        @pl.when(qi % group == left)
        def _():
            for t in range(left):
                chunk(qi - left + t, diagonal=False, first=peel_first and t == 0)
            chunk(qi, diagonal=True, first=peel_first and left == 0)
            finish()


def _diff_attention(qk, p, lam_rows, subln_w, batch, seq, heads, v_col, g_col, tq=512):
    tq = min(tq, seq)
    nq = seq // tq
    return pl.pallas_call(
        functools.partial(_diff_attn_kernel, group=min(GROUP, nq), n_q_blocks=nq),
        grid=(batch, heads, nq),
        in_specs=[pl.BlockSpec((tq, VALUE_DIM), lambda b, h, i: (b * nq + i, h)),
                  pl.BlockSpec((seq, VALUE_DIM), lambda b, h, i: (b, heads + h)),
                  pl.BlockSpec((seq, VALUE_DIM), lambda b, h, i: (b, v_col + h)),
                  pl.BlockSpec((tq, VALUE_DIM), lambda b, h, i: (b * nq + i, g_col + h)),
                  pl.BlockSpec((SUBLANES, HEAD_DIM), lambda b, h, i: (0, 0)),
                  pl.BlockSpec((1, VALUE_DIM), lambda b, h, i: (0, 0))],
        out_specs=pl.BlockSpec((tq, VALUE_DIM), lambda b, h, i: (b * nq + i, h)),
        out_shape=jax.ShapeDtypeStruct((batch * seq, heads * VALUE_DIM), BF16),
        scratch_shapes=[pltpu.VMEM((2, tq, LANES), F32),
                        pltpu.VMEM((2, tq, LANES), F32),
                        pltpu.VMEM((2, tq, VALUE_DIM), F32)],
        compiler_params=_cparams(("parallel", "parallel", "arbitrary")),
        name="diff_attention",
    )(qk, qk, p, p, lam_rows, subln_w.reshape(1, VALUE_DIM))


def _sconv_kernel(b_ref, c_ref, x_ref, g_ref, w_ref, o_ref, tail):
    ts = b_ref.shape[0]

    @pl.when(pl.program_id(1) == 0)
    def _():
        tail[...] = jnp.zeros_like(tail)

    z = c_ref[...].astype(F32) * x_ref[...].astype(F32)
    conv = _causal_taps(z, tail[...], [w_ref[j:j + 1, :] for j in range(SCONV_K)])
    tail[...] = z[ts - SUBLANES:, :]
    o_ref[...] = (b_ref[...].astype(F32) * conv * _silu(g_ref[...].astype(F32))).astype(o_ref.dtype)


def _short_conv(p, conv_w, batch, seq, width, first_col, ts=256):
    ts = min(ts, seq)
    nt = seq // ts
    spec = lambda c: pl.BlockSpec((ts, width), lambda b, t: (b * nt + t, first_col + c))
    return pl.pallas_call(
        _sconv_kernel,
        grid=(batch, nt),
        in_specs=[spec(0), spec(1), spec(2), spec(3),
                  pl.BlockSpec((SCONV_K, width), lambda b, t: (0, 0))],
        out_specs=pl.BlockSpec((ts, width), lambda b, t: (b * nt + t, 0)),
        out_shape=jax.ShapeDtypeStruct((batch * seq, width), BF16),
        scratch_shapes=[pltpu.VMEM((SUBLANES, width), F32)],
        compiler_params=_cparams(("parallel", "arbitrary")),
        name="short_conv",
    )(p, p, p, p, conv_w)


def _merge_kernel(ya_ref, yb_ref, yc_ref, wa_ref, wb_ref, wc_ref, ga_ref, gb_ref, gc_ref,
                  ba_ref, bb_ref, bc_ref, o_ref):
    out = None
    for y_ref, w_ref, gl_ref, b_ref in ((ya_ref, wa_ref, ga_ref, ba_ref),
                                        (yb_ref, wb_ref, gb_ref, bb_ref),
                                        (yc_ref, wc_ref, gc_ref, bc_ref)):
        gate = _sigmoid(gl_ref[...].astype(F32) + b_ref[...])
        term = gate * jnp.dot(y_ref[...], w_ref[...], preferred_element_type=F32)
        out = term if out is None else out + term
    o_ref[...] = out.astype(o_ref.dtype)


def _merge(ya, yb, yc, w_branch, p, gate_b, gl_col, tm=1024, tn=512):
    m, width = ya.shape
    d = w_branch.shape[1]
    tm, tn = min(tm, m), min(tn, d)
    y_spec = pl.BlockSpec((tm, width), lambda i, j: (i, 0))
    w_spec = lambda br: pl.BlockSpec((width, tn), lambda i, j: (br, j))
    g_spec = lambda br: pl.BlockSpec((tm, tn), lambda i, j: (i, (gl_col + br * d) // tn + j))
    b_spec = lambda br: pl.BlockSpec((None, 1, tn), lambda i, j: (br, 0, j))
    gate_b3 = gate_b.reshape(N_BRANCHES, 1, d)
    return pl.pallas_call(
        _merge_kernel,
        grid=(m // tm, d // tn),
        in_specs=[y_spec, y_spec, y_spec, w_spec(0), w_spec(1), w_spec(2),
                  g_spec(0), g_spec(1), g_spec(2), b_spec(0), b_spec(1), b_spec(2)],
        out_specs=pl.BlockSpec((tm, tn), lambda i, j: (i, j)),
        out_shape=jax.ShapeDtypeStruct((m, d), BF16),
        compiler_params=_cparams(("parallel", "parallel")),
        name="gated_merge",
    )(ya, yb, yc, w_branch, w_branch, w_branch, p, p, p, gate_b3, gate_b3, gate_b3)


def kernel(x, positions, norm_w, w_in, gate_b, conv_a_w, conv_a_b, w_rg, b_rg, w_ig, b_ig,
           lru_lambda, lam_q1, lam_k1, lam_q2, lam_k2, subln_w, conv_c_w, w_branch, w_out,
           final_norm_w):
    batch, seq, d = x.shape
    depth = norm_w.shape[0]
    bw = w_branch.shape[2]
    heads = bw // VALUE_DIM
    m = batch * seq
    assert w_in.shape[2] == N_BRANCH_SLICES * bw + N_BRANCHES * d
    assert seq % SUBLANES == 0 and bw % VALUE_DIM == 0

    w_branch_rows = w_branch.reshape(depth, N_BRANCHES * bw, d)
    w_in_l = w_in[0].astype(BF16)
    w_branch_l = w_branch_rows[0].astype(BF16)
    w_out_l = w_out[0].astype(BF16)
    w_gates_bf = jnp.concatenate([w_rg, w_ig], axis=-1).astype(BF16)

    cos, sin = _rope_tables(positions)
    xf = x.reshape(m, d)
    xw, ssq = _norm_prep(xf, norm_w[0])
    for l in range(depth):
        lam_init = 0.8 - 0.6 * math.exp(-0.3 * l)
        lam_rows = jnp.concatenate(
            [lam_q1[l][None], lam_k1[l][None], lam_q2[l][None], lam_k2[l][None],
             jnp.full((1, HEAD_DIM), lam_init, F32), jnp.zeros((SUBLANES - 5, HEAD_DIM), F32)], axis=0)

        last = l == depth - 1
        next_srcs = [] if last else [(w_in, CAST_TILE_W_IN), (w_branch_rows, CAST_TILE_SMALL),
                                     (w_out, CAST_TILE_SMALL)]
        p, next_w = _in_proj(xw, ssq, w_in_l, skip_start=2 * bw, skip_cols=2 * bw,
                             cast_srcs=next_srcs, cast_layer=l + 1)
        qk = _qk_proj_rope(xw, ssq, w_in_l, col_start=2 * bw, q_cols=bw, cos=cos, sin=sin)

        ya = _rglru(p, conv_a_w[l], conv_a_b[l], w_gates_bf[l], b_rg[l], b_ig[l], lru_lambda[l],
                    batch, seq, bw)
        yb = _diff_attention(qk, p, lam_rows, subln_w[l], batch, seq, heads,
                             v_col=2 * heads, g_col=3 * heads)
        yc = _short_conv(p, conv_c_w[l], batch, seq, bw, first_col=4)
        merged = _merge(ya, yb, yc, w_branch_l, p, gate_b[l], gl_col=8 * bw)
        if last:
            xf = _out_proj_residual(merged, w_out_l, xf)
        else:
            xf, xw, ssq = _out_proj_residual(merged, w_out_l, xf, next_norm_w=norm_w[l + 1])
            w_in_l, w_branch_l, w_out_l = next_w
    return _rmsnorm(xf, final_norm_w, F32).reshape(batch, seq, d)
```
